```python
import jax, jax.numpy as jnp
from jax import lax
import numpy as np

D_MODEL = 1024
BATCH = 8
SEQ = 4096
DEPTH = 1
DEC_BATCH = 32
DEC_SEQ = 1
PAST_LEN = 16384
PAGE_SIZE = 128

SSD_EXPAND = 2
D_SSD = SSD_EXPAND * D_MODEL
SSD_HEAD_DIM = 64
SSD_HEADS = D_SSD // SSD_HEAD_DIM
SSD_GROUPS = 4
SSD_STATE = 128
CONV_WIDTH = 4
SSD_CHUNK = 128
D_XBC = D_SSD + 2 * SSD_GROUPS * SSD_STATE

ATT_HEAD_DIM = 64
ATT_SLOTS = 8
ATT_GROUPS = ((128, 1), (512, 4), (2048, 16))
N_ATT_GROUPS = 3
ATT_HEADS = ATT_SLOTS * N_ATT_GROUPS
D_ATT_QK = ATT_HEADS * ATT_HEAD_DIM
D_ATT = ATT_SLOTS * ATT_HEAD_DIM
ATT_SCALE = ATT_HEAD_DIM ** -0.5
ROPE_THETA = 10000.0
EPS = 1e-6

N_IN = D_SSD + D_XBC + SSD_HEADS + 3 * D_ATT_QK + D_ATT + 2 * D_MODEL

kernel_name = 'ssd_dilated_swa_gated_hybrid_step'

F32 = jnp.float32


def _split_points():
    sizes = [D_SSD, D_XBC, SSD_HEADS, D_ATT_QK, D_ATT_QK, D_ATT_QK, D_ATT, D_MODEL, D_MODEL]
    return [int(v) for v in np.cumsum(sizes)[:-1]]


def _rmsnorm(x, w):
    xf = x.astype(F32)
    y = xf * lax.rsqrt(jnp.mean(xf * xf, axis=-1, keepdims=True) + EPS)
    return (y * w.astype(F32)).astype(x.dtype)


def _rope(x, pos):
    half = x.shape[-1] // 2
    inv = ROPE_THETA ** (-jnp.arange(half, dtype=F32) / half)
    ang = pos.astype(F32)[:, None] * inv[None, :]
    cos = jnp.cos(ang)[None, :, None, :]
    sin = jnp.sin(ang)[None, :, None, :]
    xf = x.astype(F32)
    x1, x2 = xf[..., :half], xf[..., half:]
    return jnp.concatenate([x1 * cos - x2 * sin, x2 * cos + x1 * sin], axis=-1).astype(x.dtype)


def _causal_conv(xbc, conv_state, w, b):
    xp = jnp.concatenate([conv_state.astype(xbc.dtype), xbc], axis=1)
    out = lax.conv_general_dilated(xp, w[:, None, :].astype(xp.dtype), (1,), 'VALID',
                                   dimension_numbers=('NWC', 'WIO', 'NWC'),
                                   feature_group_count=xp.shape[-1])
    out = jax.nn.silu(out + b.astype(out.dtype))
    return out, xp[:, -(CONV_WIDTH - 1):]


def _ssd_chunked(x, dt, A, Bm, Cm, h0):
    b, T, H, P = x.shape
    G, N, Q = SSD_GROUPS, SSD_STATE, SSD_CHUNK
    E = H // G
    pad = (-T) % Q
    Tp = T + pad
    nc = Tp // Q
    xf = jnp.pad(x.astype(F32), ((0, 0), (0, pad), (0, 0), (0, 0)))
    dtp = jnp.pad(dt, ((0, 0), (0, pad), (0, 0)))
    Bc = jnp.pad(Bm.astype(F32), ((0, 0), (0, pad), (0, 0), (0, 0))).reshape(b, nc, Q, G, N)
    Cc = jnp.pad(Cm.astype(F32), ((0, 0), (0, pad), (0, 0), (0, 0))).reshape(b, nc, Q, G, N)
    xdt = (xf * dtp[..., None]).reshape(b, nc, Q, G, E, P)
    a = (dtp * A).reshape(b, nc, Q, G, E)
    acum = jnp.cumsum(a, axis=2)
    seg = acum[:, :, :, None] - acum[:, :, None, :]
    causal = jnp.tril(jnp.ones((Q, Q), dtype=bool))
    Lmat = jnp.exp(jnp.where(causal[:, :, None, None], seg, -jnp.inf))
    CB = jnp.einsum('bclgn,bcsgn->bclsg', Cc, Bc)
    y_diag = jnp.einsum('bclsg,bclsge,bcsgep->bclgep', CB, Lmat, xdt)
    decay_states = jnp.exp(acum[:, :, -1:] - acum)
    states = jnp.einsum('bclgn,bclge,bclgep->bcgepn', Bc, decay_states, xdt)
    chunk_decay = jnp.exp(acum[:, :, -1])

    def step(h, inp):
        s, dec = inp
        return dec[..., None, None] * h + s, h

    h_init = h0.astype(F32).reshape(b, G, E, P, N)
    h_final, h_prev = lax.scan(step, h_init, (jnp.moveaxis(states, 1, 0), jnp.moveaxis(chunk_decay, 1, 0)))
    h_prev = jnp.moveaxis(h_prev, 0, 1)
    y_off = jnp.einsum('bclgn,bcgepn,bclge->bclgep', Cc, h_prev, jnp.exp(acum))
    y = (y_diag + y_off).reshape(b, Tp, H, P)[:, :T]
    return y, h_final.reshape(b, H, P, N)


def _dilated_attn_prompt(q, k, v, window, dil):
    b, S, h, hd = q.shape
    wk = window // dil
    L = S // dil
    nb = -(-L // wk)
    Lp = nb * wk

    def blocks(t):
        t = t.reshape(b, L, dil, h, hd).transpose(0, 2, 1, 3, 4)
        t = jnp.pad(t, ((0, 0), (0, 0), (0, Lp - L), (0, 0), (0, 0)))
        return t.reshape(b, dil, nb, wk, h, hd)

    def with_prev(t):
        prev = jnp.pad(t[:, :, :-1], ((0, 0), (0, 0), (1, 0), (0, 0), (0, 0), (0, 0)))
        return jnp.concatenate([prev, t], axis=3)

    qb = blocks(q)
    kc, vc = with_prev(blocks(k)), with_prev(blocks(v))
    s = jnp.einsum('brnqhd,brnkhd->brnhqk', qb, kc).astype(F32) * ATT_SCALE
    qi = jnp.arange(wk)[:, None]
    ki = jnp.arange(2 * wk)[None, :]
    dist = wk + qi - ki
    band = (dist >= 0) & (dist <= wk)
    first = jnp.arange(nb)[:, None, None] == 0
    mask = band[None] & (jnp.logical_not(first) | (ki >= wk)[None])
    s = jnp.where(mask[None, None, :, None], s, -jnp.inf)
    lse = jax.nn.logsumexp(s, axis=-1)
    p = jnp.exp(s - lse[..., None])
    o = jnp.einsum('brnhqk,brnkhd->brnqhd', p.astype(v.dtype), vc)
    o = o.reshape(b, dil, Lp, h, hd)[:, :, :L].transpose(0, 2, 1, 3, 4).reshape(b, S, h, hd)
    lse = lse.transpose(0, 1, 2, 4, 3).reshape(b, dil, Lp, h)[:, :, :L].transpose(0, 2, 1, 3).reshape(b, S, h)
    return o, lse


def _dilated_attn_sample(q, k, v, buf, window, dil):
    b, T, h, hd = q.shape
    Wb = buf.shape[1]
    kv_all = jnp.concatenate([buf.astype(k.dtype), jnp.stack([k, v], axis=2)], axis=1)
    nk = window // dil + 1
    idx = Wb + jnp.arange(T)[:, None] - dil * jnp.arange(nk)[None, :]
    valid = idx >= 0
    kvg = kv_all[:, jnp.maximum(idx, 0)]
    s = jnp.einsum('bthd,btkhd->bthk', q, kvg[:, :, :, 0]).astype(F32) * ATT_SCALE
    s = jnp.where(valid[None, :, None, :], s, -jnp.inf)
    lse = jax.nn.logsumexp(s, axis=-1)
    p = jnp.exp(s - lse[..., None])
    o = jnp.einsum('bthk,btkhd->bthd', p.astype(v.dtype), kvg[:, :, :, 1])
    return o, lse, kv_all


def _layer(x, pos, conv_state, ssm_state, kv_bufs, norm_w, w_in, conv_w, conv_b, dt_bias, a_log,
           d_skip, ssd_norm_w, q_norm_w, k_norm_w, w_ssd_br, w_att_br, w_out):
    b, t, _ = x.shape
    hn = _rmsnorm(x, norm_w)
    u = hn @ w_in
    z_ssd, xbc, dt_raw, q, k, v, z_att, g_ssd, g_att = jnp.split(u, _split_points(), axis=-1)

    xbc, new_conv = _causal_conv(xbc, conv_state, conv_w, conv_b)
    xs, Bm, Cm = jnp.split(xbc, [D_SSD, D_SSD + SSD_GROUPS * SSD_STATE], axis=-1)
    xh = xs.reshape(b, t, SSD_HEADS, SSD_HEAD_DIM)
    dt = jax.nn.softplus(dt_raw.astype(F32) + dt_bias.astype(F32))
    A = -jnp.exp(a_log.astype(F32))
    y, new_ssm = _ssd_chunked(xh, dt, A, Bm.reshape(b, t, SSD_GROUPS, SSD_STATE),
                              Cm.reshape(b, t, SSD_GROUPS, SSD_STATE), ssm_state)
    y = y + d_skip.astype(F32)[:, None] * xh.astype(F32)
    yg = (y.reshape(b, t, D_SSD) * jax.nn.silu(z_ssd.astype(F32))).reshape(b, t, SSD_GROUPS, D_SSD // SSD_GROUPS)
    yg = yg * lax.rsqrt(jnp.mean(yg * yg, axis=-1, keepdims=True) + EPS)
    y = yg.reshape(b, t, D_SSD) * ssd_norm_w.astype(F32)
    br_ssd = (y.astype(x.dtype) @ w_ssd_br).astype(F32)

    q = _rope(_rmsnorm(q.reshape(b, t, ATT_HEADS, ATT_HEAD_DIM), q_norm_w), pos)
    k = _rope(_rmsnorm(k.reshape(b, t, ATT_HEADS, ATT_HEAD_DIM), k_norm_w), pos)
    v = v.reshape(b, t, ATT_HEADS, ATT_HEAD_DIM)
    outs, lses, new_bufs = [], [], []
    for g, (win, dil) in enumerate(ATT_GROUPS):
        sl = slice(g * ATT_SLOTS, (g + 1) * ATT_SLOTS)
        qg, kg, vg = q[:, :, sl], k[:, :, sl], v[:, :, sl]
        if kv_bufs is None:
            o, lse = _dilated_attn_prompt(qg, kg, vg, win, dil)
            kv_all = jnp.stack([kg, vg], axis=2)
        else:
            o, lse, kv_all = _dilated_attn_sample(qg, kg, vg, kv_bufs[g], win, dil)
        outs.append(o)
        lses.append(lse)
        new_bufs.append(kv_all[:, -min(win, kv_all.shape[1]):])
    alpha = jax.nn.softmax(jnp.stack(lses, axis=-1), axis=-1)
    o = jnp.einsum('bthdg,bthg->bthd', jnp.stack(outs, axis=-1).astype(F32), alpha)
    o = o.reshape(b, t, D_ATT) * jax.nn.silu(z_att.astype(F32))
    br_att = (o.astype(x.dtype) @ w_att_br).astype(F32)

    mix = jax.nn.sigmoid(g_ssd.astype(F32)) * br_ssd + jax.nn.sigmoid(g_att.astype(F32)) * br_att
    out = x + mix.astype(x.dtype) @ w_out
    return out, new_conv, new_ssm, new_bufs


def setup_inputs(seed: int = 0) -> dict:
    key = jax.random.key(seed)
    ks = jax.random.split(key, 24)
    nrm = jax.random.normal
    win_rows = [min(w, PAST_LEN) for (w, _) in ATT_GROUPS]
    dt0 = jnp.exp(jax.random.uniform(ks[0], (DEPTH, SSD_HEADS)) * (np.log(0.1) - np.log(0.001)) + np.log(0.001))
    return {
        'x_prompt': nrm(ks[1], (BATCH, SEQ, D_MODEL), F32),
        'x_sample': nrm(ks[2], (DEC_BATCH, DEC_SEQ, D_MODEL), F32),
        'state_conv': nrm(ks[3], (DEPTH, DEC_BATCH, CONV_WIDTH - 1, D_XBC), F32),
        'state_ssm': 0.1 * nrm(ks[4], (DEPTH, DEC_BATCH, SSD_HEADS, SSD_HEAD_DIM, SSD_STATE), F32),
        'cache_kv_w128': nrm(ks[5], (DEPTH, DEC_BATCH, win_rows[0], 2, ATT_SLOTS, ATT_HEAD_DIM), F32),
        'cache_kv_w512': nrm(ks[6], (DEPTH, DEC_BATCH, win_rows[1], 2, ATT_SLOTS, ATT_HEAD_DIM), F32),
        'cache_kv_w2048': nrm(ks[7], (DEPTH, DEC_BATCH, win_rows[2], 2, ATT_SLOTS, ATT_HEAD_DIM), F32),
        'norm_w': 1.0 + 0.02 * nrm(ks[8], (DEPTH, D_MODEL), F32),
        'w_in': nrm(ks[9], (DEPTH, D_MODEL, N_IN), F32) * D_MODEL ** -0.5,
        'conv_w': nrm(ks[10], (DEPTH, CONV_WIDTH, D_XBC), F32) * CONV_WIDTH ** -0.5,
        'conv_b': 0.01 * nrm(ks[11], (DEPTH, D_XBC), F32),
        'dt_bias': dt0 + jnp.log(-jnp.expm1(-dt0)),
        'a_log': jnp.log(jax.random.uniform(ks[12], (DEPTH, SSD_HEADS), F32, 1.0, 16.0)),
        'd_skip': 1.0 + 0.1 * nrm(ks[13], (DEPTH, SSD_HEADS), F32),
        'ssd_norm_w': 1.0 + 0.02 * nrm(ks[14], (DEPTH, D_SSD), F32),
        'q_norm_w': 1.0 + 0.02 * nrm(ks[15], (DEPTH, ATT_HEAD_DIM), F32),
        'k_norm_w': 1.0 + 0.02 * nrm(ks[16], (DEPTH, ATT_HEAD_DIM), F32),
        'w_ssd_br': nrm(ks[17], (DEPTH, D_SSD, D_MODEL), F32) * D_SSD ** -0.5,
        'w_att_br': nrm(ks[18], (DEPTH, D_ATT, D_MODEL), F32) * D_ATT ** -0.5,
        'w_out': nrm(ks[19], (DEPTH, D_MODEL, D_MODEL), F32) * D_MODEL ** -0.5,
    }


def reference(x_prompt, x_sample, state_conv, state_ssm, cache_kv_w128, cache_kv_w512, cache_kv_w2048,
              norm_w, w_in, conv_w, conv_b, dt_bias, a_log, d_skip, ssd_norm_w, q_norm_w, k_norm_w,
              w_ssd_br, w_att_br, w_out):
    yp, ys = x_prompt, x_sample
    bp, sp = x_prompt.shape[0], x_prompt.shape[1]
    pos_p = jnp.arange(sp, dtype=jnp.int32)
    pos_s = PAST_LEN + jnp.arange(x_sample.shape[1], dtype=jnp.int32)
    conv_p, ssm_p, kv128_p, kv512_p, kv2048_p = [], [], [], [], []
    conv_s, ssm_s, kv128_s, kv512_s, kv2048_s = [], [], [], [], []
    for l in range(DEPTH):
        wl = (norm_w[l], w_in[l], conv_w[l], conv_b[l], dt_bias[l], a_log[l], d_skip[l], ssd_norm_w[l],
              q_norm_w[l], k_norm_w[l], w_ssd_br[l], w_att_br[l], w_out[l])
        conv0 = jnp.zeros((bp, CONV_WIDTH - 1, D_XBC), yp.dtype)
        ssm0 = jnp.zeros((bp, SSD_HEADS, SSD_HEAD_DIM, SSD_STATE), F32)
        yp, c, s, kv = _layer(yp, pos_p, conv0, ssm0, None, *wl)
        conv_p.append(c); ssm_p.append(s)
        kv128_p.append(kv[0]); kv512_p.append(kv[1]); kv2048_p.append(kv[2])
        ys, c, s, kv = _layer(ys, pos_s, state_conv[l], state_ssm[l],
                              [cache_kv_w128[l], cache_kv_w512[l], cache_kv_w2048[l]], *wl)
        conv_s.append(c); ssm_s.append(s)
        kv128_s.append(kv[0]); kv512_s.append(kv[1]); kv2048_s.append(kv[2])
    return (yp, ys,
            jnp.stack(conv_p), jnp.stack(ssm_p), jnp.stack(kv128_p), jnp.stack(kv512_p), jnp.stack(kv2048_p),
            jnp.stack(conv_s), jnp.stack(ssm_s), jnp.stack(kv128_s), jnp.stack(kv512_s), jnp.stack(kv2048_s))
```

```python
import functools

import numpy as np
import jax
import jax.numpy as jnp
from jax import lax
from jax.experimental import pallas as pl
from jax.experimental.pallas import tpu as pltpu

F32 = jnp.float32
BF16 = jnp.bfloat16

D_MODEL = 1024
D_SSD = 2048
SSD_HEADS = 32
SSD_HEAD_DIM = 64
SSD_GROUPS = 4
SSD_STATE = 128
CONV_WIDTH = 4
D_BC = SSD_GROUPS * SSD_STATE
D_XBC = D_SSD + 2 * D_BC
ATT_HEAD_DIM = 64
ATT_SLOTS = 8
ATT_GROUPS = ((128, 1), (512, 4), (2048, 16))
D_ATT = ATT_SLOTS * ATT_HEAD_DIM
D_ATT_QK = 3 * D_ATT
ATT_SCALE = ATT_HEAD_DIM ** -0.5
ROPE_THETA = 10000.0
EPS = 1e-6
PAST_LEN = 16384

N_U = 12288
U_Z, U_X, U_B, U_C, U_Q, U_K, U_V, U_ZATT, U_GSSD, U_GATT = (
    0, 2048, 4096, 4608, 5120, 6656, 8192, 9728, 10240, 11264)

LANES = 128
SUBLANES = 8
SSD_CHUNK = 128
ATT_BLOCK = 128
VMEM_LIMIT = 56 * 1024 * 1024


def _split_dot(a, m, terms, left=False):
    out = None
    r = a
    for t in range(terms):
        p = r.astype(BF16)
        d = (jnp.dot(m, p, preferred_element_type=F32) if left
             else jnp.dot(p, m, preferred_element_type=F32))
        out = d if out is None else out + d
        if t + 1 < terms:
            r = r - p.astype(F32)
    return out


def _sigmoid(x):
    return 1.0 / (1.0 + jnp.exp(-x))


def _silu(x):
    return x * _sigmoid(x)


def _softplus(x):
    return jnp.maximum(x, 0.0) + jnp.log1p(jnp.exp(-jnp.abs(x)))


def _inproj_kernel(x_ref, nw_ref, w_ref, wdt_ref, u_ref, dt_ref, hn_ref):
    @pl.when(pl.program_id(1) == 0)
    def _():
        x = x_ref[...]
        ms = jnp.mean(x * x, axis=-1, keepdims=True)
        hn = (x * lax.rsqrt(ms + EPS) * nw_ref[...]).astype(BF16)
        hn_ref[...] = hn
        dt_ref[...] = jnp.dot(hn, wdt_ref[...], preferred_element_type=F32)

    u_ref[...] = jnp.dot(hn_ref[...], w_ref[...], preferred_element_type=F32).astype(u_ref.dtype)


def _inproj(x2d, norm_w, w_main, w_dt, out_dtype, tm, tn):
    m = x2d.shape[0]
    return pl.pallas_call(
        _inproj_kernel,
        grid=(m // tm, N_U // tn),
        in_specs=[
            pl.BlockSpec((tm, D_MODEL), lambda i, j: (i, 0)),
            pl.BlockSpec((1, D_MODEL), lambda i, j: (0, 0)),
            pl.BlockSpec((D_MODEL, tn), lambda i, j: (0, j)),
            pl.BlockSpec((D_MODEL, LANES), lambda i, j: (0, 0)),
        ],
        out_specs=[
            pl.BlockSpec((tm, tn), lambda i, j: (i, j)),
            pl.BlockSpec((tm, LANES), lambda i, j: (i, 0)),
        ],
        out_shape=[
            jax.ShapeDtypeStruct((m, N_U), out_dtype),
            jax.ShapeDtypeStruct((m, LANES), F32),
        ],
        scratch_shapes=[pltpu.VMEM((tm, D_MODEL), BF16)],
        compiler_params=pltpu.CompilerParams(
            dimension_semantics=("parallel", "arbitrary"), vmem_limit_bytes=VMEM_LIMIT),
        name="inproj",
    )(x2d, norm_w, w_main, w_dt)


def _ssd_kernel(z_ref, x_ref, bm_ref, cm_ref, dt_ref, cs_ref, h0_ref, cw_ref, cb_ref, dtb_ref,
                a_ref, dsk_ref, nw_ref, e_ref, wbr_ref, y_ref, cso_ref, ho_ref,
                xin_ref, dtin_ref, ht_ref, yd_ref, *pad_refs, q, t, nc):
    c = pl.program_id(1)
    tail = CONV_WIDTH - 1
    base = SUBLANES
    hp = D_SSD
    gw = D_SSD // SSD_GROUPS
    e_per_g = SSD_HEADS // SSD_GROUPS

    @pl.when(c == 0)
    def _():
        xin_ref[0:base, :] = jnp.zeros((base, D_XBC), F32)
        xin_ref[base - tail:base, :] = cs_ref[0]
        ht_ref[...] = h0_ref[0].reshape(hp, SSD_STATE).T

    if t < q:
        xin_ref[base:base + q, :] = jnp.zeros((q, D_XBC), F32)
        dtin_ref[...] = jnp.zeros((q, LANES), F32)
        (zin_ref,) = pad_refs
        zin_ref[...] = jnp.zeros((q, D_SSD), F32)
        zin_ref[0:t, :] = z_ref[0].astype(F32)
    xin_ref[base:base + t, 0:D_SSD] = x_ref[0].astype(F32)
    xin_ref[base:base + t, D_SSD:D_SSD + D_BC] = bm_ref[0].astype(F32)
    xin_ref[base:base + t, D_SSD + D_BC:D_XBC] = cm_ref[0].astype(F32)
    dtin_ref[0:t, :] = dt_ref[0]

    def conv(lo, hi):
        acc = cb_ref[:, lo:hi]
        for j in range(CONV_WIDTH):
            acc = acc + cw_ref[j:j + 1, lo:hi] * xin_ref[base - tail + j:base - tail + j + q, lo:hi]
        return _silu(acc)

    dt = _softplus(dtin_ref[...] + dtb_ref[...])
    if t < q:
        rows = lax.broadcasted_iota(jnp.int32, (q, LANES), 0)
        dt = jnp.where(rows < t, dt, 0.0)
    a = dt * a_ref[...]
    ri = lax.broadcasted_iota(jnp.int32, (q, q), 0)
    ci = lax.broadcasted_iota(jnp.int32, (q, q), 1)
    causal = ri >= ci
    tril = jnp.where(causal, 1.0, 0.0).astype(BF16)
    acum = _split_dot(a, tril, 3, left=True)
    acum_t = acum.T
    a_last = acum[q - 1:q, :]
    e_mat = e_ref[...]
    dt_e = _split_dot(dt, e_mat, 3)
    ea_e = _split_dot(jnp.exp(acum), e_mat, 3)
    ds_e = _split_dot(jnp.exp(a_last - acum), e_mat, 3)

    bmat = conv(D_SSD, D_SSD + D_BC)
    cmat = conv(D_SSD + D_BC, D_XBC).astype(BF16)
    bt = bmat.T.astype(BF16)

    yn_parts = []
    for g in range(SSD_GROUPS):
        gs = slice(g * gw, (g + 1) * gw)
        xs = conv(g * gw, (g + 1) * gw)
        xdt = xs * dt_e[:, gs]
        xdt_b = xdt.astype(BF16)
        cg = cmat[:, g * SSD_STATE:(g + 1) * SSD_STATE]
        bgt = bt[g * SSD_STATE:(g + 1) * SSD_STATE, :]
        cb = jnp.dot(cg, bgt, preferred_element_type=F32)
        ht_g = ht_ref[:, gs]
        y_off = jnp.dot(cg, ht_g.astype(BF16), preferred_element_type=F32)
        for e in range(e_per_g):
            h = g * e_per_g + e
            seg = acum[:, h:h + 1] - acum_t[h:h + 1, :]
            lmat = jnp.exp(jnp.where(causal, seg, -jnp.inf))
            mh = (cb * lmat).astype(BF16)
            hs = slice(e * SSD_HEAD_DIM, (e + 1) * SSD_HEAD_DIM)
            yd_ref[:, h * SSD_HEAD_DIM:(h + 1) * SSD_HEAD_DIM] = jnp.dot(
                mh, xdt_b[:, hs], preferred_element_type=F32)
        y = yd_ref[:, gs] + y_off * ea_e[:, gs] + dsk_ref[:, gs] * xs
        xw = (xdt * ds_e[:, gs]).astype(BF16)
        ht_ref[:, gs] = ht_g * ea_e[q - 1:q, gs] + jnp.dot(bgt, xw, preferred_element_type=F32)
        zg = z_ref[0][:, gs].astype(F32) if t == q else zin_ref[:, gs]
        yg = y * _silu(zg)
        ms = jnp.mean(yg * yg, axis=-1, keepdims=True)
        yn_parts.append((yg * lax.rsqrt(ms + EPS) * nw_ref[:, gs]).astype(BF16))
    yn = jnp.concatenate(yn_parts, axis=-1)
    br = jnp.dot(yn, wbr_ref[...], preferred_element_type=F32)
    y_ref[0] = br[0:t, :].astype(y_ref.dtype)

    @pl.when(c == nc - 1)
    def _():
        cso_ref[0] = xin_ref[base + t - tail:base + t, :]
        ho_ref[0] = ht_ref[...].T.reshape(SSD_HEADS, SSD_HEAD_DIM, SSD_STATE)

    if t == q:
        xin_ref[base - tail:base, :] = xin_ref[base + q - tail:base + q, :]


def _const_spec(shape):
    return pl.BlockSpec(shape, lambda *_: (0,) * len(shape))


def _ssd(u3, dt3, conv_state, ssm_state, p, out_dtype):
    b, s, _ = u3.shape
    q = SSD_CHUNK
    t = min(q, s)
    nc = s // t
    assert s == nc * t and (t == q or nc == 1)

    def ucol(width, off):
        return pl.BlockSpec((1, t, width), lambda bi, ci: (bi, ci, off // width))

    scratch = [
        pltpu.VMEM((SUBLANES + q, D_XBC), F32),
        pltpu.VMEM((q, LANES), F32),
        pltpu.VMEM((SSD_STATE, D_SSD), F32),
        pltpu.VMEM((q, D_SSD), F32),
    ]
    if t < q:
        scratch.append(pltpu.VMEM((q, D_SSD), F32))
    return pl.pallas_call(
        functools.partial(_ssd_kernel, q=q, t=t, nc=nc),
        grid=(b, nc),
        in_specs=[
            ucol(D_SSD, U_Z), ucol(D_SSD, U_X), ucol(D_BC, U_B), ucol(D_BC, U_C),
            pl.BlockSpec((1, t, LANES), lambda bi, ci: (bi, ci, 0)),
            pl.BlockSpec((1, CONV_WIDTH - 1, D_XBC), lambda bi, ci: (bi, 0, 0)),
            pl.BlockSpec((1, SSD_HEADS, SSD_HEAD_DIM, SSD_STATE), lambda bi, ci: (bi, 0, 0, 0)),
            _const_spec((CONV_WIDTH, D_XBC)), _const_spec((1, D_XBC)),
            _const_spec((1, LANES)), _const_spec((1, LANES)),
            _const_spec((1, D_SSD)), _const_spec((1, D_SSD)),
            _const_spec((LANES, D_SSD)), _const_spec((D_SSD, D_MODEL)),
        ],
        out_specs=[
            pl.BlockSpec((1, t, D_MODEL), lambda bi, ci: (bi, ci, 0)),
            pl.BlockSpec((1, CONV_WIDTH - 1, D_XBC), lambda bi, ci: (bi, 0, 0)),
            pl.BlockSpec((1, SSD_HEADS, SSD_HEAD_DIM, SSD_STATE), lambda bi, ci: (bi, 0, 0, 0)),
        ],
        out_shape=[
            jax.ShapeDtypeStruct((b, s, D_MODEL), out_dtype),
            jax.ShapeDtypeStruct((b, CONV_WIDTH - 1, D_XBC), F32),
            jax.ShapeDtypeStruct((b, SSD_HEADS, SSD_HEAD_DIM, SSD_STATE), F32),
        ],
        scratch_shapes=scratch,
        compiler_params=pltpu.CompilerParams(
            dimension_semantics=("parallel", "arbitrary"), vmem_limit_bytes=VMEM_LIMIT),
        name="ssd",
    )(u3, u3, u3, u3, dt3, conv_state, ssm_state, p["conv_w"], p["conv_b"], p["dt_bias"], p["a_neg"],
      p["d_skip"], p["ssd_norm_w"], p["head_expand"], p["w_ssd_br"])


def _norm_rope(x, w, cos, sin, seg):
    width = x.shape[-1]
    sq = x * x
    ss = jnp.concatenate(
        [_split_dot(sq[:, c * LANES:(c + 1) * LANES], seg, 2) for c in range(width // LANES)], axis=-1)
    xn = x * lax.rsqrt(ss * (1.0 / ATT_HEAD_DIM) + EPS) * w
    half = ATT_HEAD_DIM // 2
    ahead = pltpu.roll(xn, width - half, axis=1)
    behind = pltpu.roll(xn, half, axis=1)
    lane = lax.broadcasted_iota(jnp.int32, x.shape, 1)
    partner = jnp.where(lane % ATT_HEAD_DIM < half, ahead, behind)
    return xn * cos + partner * sin


def _attn_prompt_kernel(*refs, merge, nb):
    q_ref, k_ref, v_ref, cos_ref, sin_ref, qw_ref, kw_ref, seg_ref = refs[:8]
    refs = refs[8:]
    if merge:
        o0_ref, l0_ref, o1_ref, l1_ref = refs[:4]
        refs = refs[4:]
    o_ref, l_ref, kv_ref, kp_ref, vp_ref, ob_ref = refs
    n = pl.program_id(2)
    blk = ATT_BLOCK
    hd = ATT_HEAD_DIM

    @pl.when(n == 0)
    def _():
        kp_ref[...] = jnp.zeros(kp_ref.shape, BF16)
        vp_ref[...] = jnp.zeros(vp_ref.shape, BF16)

    reps = D_ATT // LANES
    cos = jnp.concatenate([cos_ref[...]] * reps, axis=-1)
    sin = jnp.concatenate([sin_ref[...]] * reps, axis=-1)
    seg = seg_ref[...]
    qn = _norm_rope(q_ref[0].astype(F32), qw_ref[...], cos, sin, seg)
    kn = _norm_rope(k_ref[0].astype(F32), kw_ref[...], cos, sin, seg)
    v = v_ref[0]
    qb = (qn * ATT_SCALE).astype(BF16)
    kb = kn.astype(BF16)
    kp = kp_ref[...]
    vp = vp_ref[...]

    ri = lax.broadcasted_iota(jnp.int32, (blk, blk), 0)
    ci = lax.broadcasted_iota(jnp.int32, (blk, blk), 1)
    keep_prev = jnp.logical_and(ci >= ri, n > 0)
    keep_cur = ci <= ri
    nt = (((1,), (1,)), ((), ()))
    for h in range(ATT_SLOTS):
        hs = slice(h * hd, (h + 1) * hd)
        s_p = lax.dot_general(qb[:, hs], kp[:, hs], nt, preferred_element_type=F32)
        s_c = lax.dot_general(qb[:, hs], kb[:, hs], nt, preferred_element_type=F32)
        s_p = jnp.where(keep_prev, s_p, -jnp.inf)
        s_c = jnp.where(keep_cur, s_c, -jnp.inf)
        m = jnp.maximum(jnp.max(s_p, axis=-1, keepdims=True), jnp.max(s_c, axis=-1, keepdims=True))
        p_p = jnp.exp(s_p - m)
        p_c = jnp.exp(s_c - m)
        den = jnp.sum(p_p, axis=-1, keepdims=True) + jnp.sum(p_c, axis=-1, keepdims=True)
        o_h = (jnp.dot(p_p.astype(BF16), vp[:, hs], preferred_element_type=F32)
               + jnp.dot(p_c.astype(BF16), v[:, hs], preferred_element_type=F32)) / den
        lse = m + jnp.log(den)
        if merge:
            ls = slice(h * 16, h * 16 + 1)
            la, lb = l0_ref[0][:, ls], l1_ref[0][:, ls]
            top = jnp.maximum(jnp.maximum(la, lb), lse)
            wa, wb, wc = jnp.exp(la - top), jnp.exp(lb - top), jnp.exp(lse - top)
            o_h = (wa * o0_ref[0][:, hs].astype(F32) + wb * o1_ref[0][:, hs].astype(F32)
                   + wc * o_h) / (wa + wb + wc)
        ob_ref[:, hs] = o_h
        l_ref[0, :, h * 16:(h + 1) * 16] = jnp.broadcast_to(lse, (blk, 16))
    o_ref[0] = ob_ref[...].astype(o_ref.dtype)

    @pl.when(n == nb - 1)
    def _():
        kv_ref[0, :, 0:D_ATT] = kn
        kv_ref[0, :, D_ATT:2 * D_ATT] = v.astype(F32)

    kp_ref[...] = kb
    vp_ref[...] = v


def _attn_prompt(u2d, b, s, gi, cos, sin, qw, kw, seg, prev):
    win, dil = ATT_GROUPS[gi]
    rows = s // dil
    nb = rows // ATT_BLOCK
    assert win // dil == ATT_BLOCK and rows == nb * ATT_BLOCK and s >= win
    ncol = N_U // D_ATT

    def ucol(off):
        return pl.BlockSpec((1, ATT_BLOCK, D_ATT),
                            lambda bi, r, n: (bi, n, r * ncol + off // D_ATT + gi))

    tok = lambda width: pl.BlockSpec((1, ATT_BLOCK, width), lambda bi, r, n: (bi, n, r))
    tab = pl.BlockSpec((ATT_BLOCK, LANES), lambda bi, r, n: (n, r))
    u3 = u2d.reshape(b, rows, dil * N_U)
    prev_specs, prev_args = [], []
    for i, arr in enumerate(prev):
        width = D_ATT if i % 2 == 0 else LANES
        prev_specs.append(tok(width))
        prev_args.append(arr.reshape(b, rows, dil * width))
    o, lse, kv = pl.pallas_call(
        functools.partial(_attn_prompt_kernel, merge=bool(prev), nb=nb),
        grid=(b, dil, nb),
        in_specs=[ucol(U_Q), ucol(U_K), ucol(U_V), tab, tab,
                  _const_spec((1, D_ATT)), _const_spec((1, D_ATT)), _const_spec((LANES, LANES))]
        + prev_specs,
        out_specs=[tok(D_ATT), tok(LANES),
                   pl.BlockSpec((1, ATT_BLOCK, 2 * D_ATT), lambda bi, r, n: (bi, 0, r))],
        out_shape=[
            jax.ShapeDtypeStruct((b, rows, dil * D_ATT), BF16),
            jax.ShapeDtypeStruct((b, rows, dil * LANES), F32),
            jax.ShapeDtypeStruct((b, ATT_BLOCK, dil * 2 * D_ATT), F32),
        ],
        scratch_shapes=[pltpu.VMEM((ATT_BLOCK, D_ATT), BF16), pltpu.VMEM((ATT_BLOCK, D_ATT), BF16),
                        pltpu.VMEM((ATT_BLOCK, D_ATT), F32)],
        compiler_params=pltpu.CompilerParams(
            dimension_semantics=("parallel", "parallel", "arbitrary"), vmem_limit_bytes=VMEM_LIMIT),
        name=f"attn_prompt_g{gi}",
    )(u3, u3, u3, cos.reshape(rows, dil * LANES), sin.reshape(rows, dil * LANES), qw, kw, seg, *prev_args)
    return (o.reshape(b * s, D_ATT), lse.reshape(b * s, LANES),
            kv.reshape(b, win, 2, ATT_SLOTS, ATT_HEAD_DIM))


def _attn_decode_kernel(u_ref, c0_ref, c1_ref, c2_ref, r0_ref, r1_ref, r2_ref, cos_ref, sin_ref,
                        w_ref, seg_ref, o_ref, k0_ref, k1_ref, k2_ref, new_ref, sem_ref):
    b = pl.program_id(0)
    strided = (c0_ref, c1_ref, c2_ref)
    raws = (r0_ref, r1_ref, r2_ref)
    outs = (k0_ref, k1_ref, k2_ref)
    ng = len(ATT_GROUPS)
    half_t = SUBLANES // 2

    shifts = []
    for g, (win, _) in enumerate(ATT_GROUPS):
        cp = pltpu.make_async_copy(raws[g].at[b, pl.ds(1, win - 1)],
                                   outs[g].at[b, pl.ds(0, win - 1)], sem_ref.at[g])
        cp.start()
        shifts.append(cp)

    us = u_ref[0]
    nq = D_ATT_QK // LANES
    seg = seg_ref[...]
    qkn = _norm_rope(us[U_Q // LANES:U_Q // LANES + 2 * nq, :], w_ref[...],
                     cos_ref[...], sin_ref[...], seg)
    qn = qkn[0:nq] * ATT_SCALE
    kn = qkn[nq:2 * nq]
    vv = us[U_V // LANES:U_V // LANES + nq, :]
    zeros4 = jnp.zeros((half_t, LANES), F32)

    outs_g, lses, news = [], [], []
    for g, (win, dil) in enumerate(ATT_GROUPS):
        gsl = slice(g * half_t, (g + 1) * half_t)
        kv3 = strided[g][0, :, 0]
        nk = kv3.shape[0]
        q8 = jnp.concatenate([qn[gsl], zeros4], axis=0)
        new8 = jnp.concatenate([kn[gsl], vv[gsl]], axis=0)
        sc = _split_dot((kv3 * q8[None]).reshape(nk * SUBLANES, LANES), seg, 3)
        sc = sc.reshape(nk, SUBLANES, LANES)
        sc_new = _split_dot(new8 * q8, seg, 3)
        m = jnp.maximum(jnp.max(sc, axis=0), sc_new)
        p = jnp.exp(sc - m[None])
        p_new = jnp.exp(sc_new - m)
        den = jnp.sum(p, axis=0) + p_new
        lse = m + jnp.log(den)
        p_sh = pltpu.roll(p.reshape(nk * SUBLANES, LANES), half_t, axis=0).reshape(nk, SUBLANES, LANES)
        acc = jnp.sum(p_sh * kv3, axis=0) + pltpu.roll(p_new, half_t, axis=0) * new8
        acc = acc / pltpu.roll(den, half_t, axis=0)
        outs_g.append(acc[half_t:SUBLANES])
        lses.append(lse[0:half_t])
        new_ref[g] = new8
        cp = pltpu.make_async_copy(new_ref.at[g], outs[g].at[b, win - 1], sem_ref.at[ng + g])
        cp.start()
        news.append(cp)

    top = jnp.maximum(jnp.maximum(lses[0], lses[1]), lses[2])
    ws = [jnp.exp(l - top) for l in lses]
    o_ref[0] = (ws[0] * outs_g[0] + ws[1] * outs_g[1] + ws[2] * outs_g[2]) / (ws[0] + ws[1] + ws[2])

    for cp in shifts + news:
        cp.wait()


def _attn_decode(us, caches, cos, sin, w24, seg):
    b = us.shape[0]
    tile = (SUBLANES, LANES)
    strided, raws, out_shapes = [], [], []
    strided_specs = []
    for cache, (win, dil) in zip(caches, ATT_GROUPS):
        assert cache.shape[1] == win and win // dil == ATT_BLOCK
        strided.append(cache.reshape(b, win // dil, dil, *tile))
        strided_specs.append(pl.BlockSpec((1, win // dil, 1) + tile, lambda bi: (bi, 0, 0, 0, 0)))
        raws.append(cache.reshape(b, win, *tile))
        out_shapes.append(jax.ShapeDtypeStruct((b, win) + tile, F32))
    any_spec = pl.BlockSpec(memory_space=pl.ANY)
    nslab = N_U // LANES
    res = pl.pallas_call(
        _attn_decode_kernel,
        grid=(b,),
        in_specs=[pl.BlockSpec((1, nslab, LANES), lambda bi: (bi, 0, 0))] + strided_specs
        + [any_spec] * 3
        + [_const_spec((1, LANES)), _const_spec((1, LANES)), _const_spec(w24.shape),
           _const_spec((LANES, LANES))],
        out_specs=[pl.BlockSpec((1, SUBLANES // 2, LANES), lambda bi: (bi, 0, 0))] + [any_spec] * 3,
        out_shape=[jax.ShapeDtypeStruct((b, SUBLANES // 2, LANES), F32)] + out_shapes,
        scratch_shapes=[pltpu.VMEM((3,) + tile, F32), pltpu.SemaphoreType.DMA((6,))],
        compiler_params=pltpu.CompilerParams(
            dimension_semantics=("arbitrary",), vmem_limit_bytes=VMEM_LIMIT),
        name="attn_decode",
    )(us.reshape(b, nslab, LANES), *strided, *raws, cos, sin, w24, seg)
    o = res[0].reshape(b, D_ATT)
    kvs = [r.reshape(b, win, 2, ATT_SLOTS, ATT_HEAD_DIM) for r, (win, _) in zip(res[1:], ATT_GROUPS)]
    return o, kvs


def _out_kernel(o_ref, z_ref, br_ref, gs_ref, ga_ref, x_ref, wa_ref, wo_ref, y_ref):
    o = o_ref[...].astype(F32) * _silu(z_ref[...].astype(F32))
    br_att = jnp.dot(o.astype(BF16), wa_ref[...], preferred_element_type=F32)
    mix = (_sigmoid(gs_ref[...].astype(F32)) * br_ref[...].astype(F32)
           + _sigmoid(ga_ref[...].astype(F32)) * br_att)
    y_ref[...] = x_ref[...] + jnp.dot(mix.astype(BF16), wo_ref[...], preferred_element_type=F32)


def _out_proj(o, u2d, br_ssd, x2d, w_att_br, w_out, tm):
    m = x2d.shape[0]
    row = lambda width, off=0: pl.BlockSpec((tm, width), lambda i: (i, off // width))
    return pl.pallas_call(
        _out_kernel,
        grid=(m // tm,),
        in_specs=[row(D_ATT), row(D_ATT, U_ZATT), row(D_MODEL), row(D_MODEL, U_GSSD),
                  row(D_MODEL, U_GATT), row(D_MODEL),
                  _const_spec((D_ATT, D_MODEL)), _const_spec((D_MODEL, D_MODEL))],
        out_specs=row(D_MODEL),
        out_shape=jax.ShapeDtypeStruct((m, D_MODEL), F32),
        compiler_params=pltpu.CompilerParams(
            dimension_semantics=("parallel",), vmem_limit_bytes=VMEM_LIMIT),
        name="out_proj",
    )(o, u2d, br_ssd, u2d, u2d, x2d, w_att_br, w_out)


def _rope_tables(pos):
    half = ATT_HEAD_DIM // 2
    inv = ROPE_THETA ** (-jnp.arange(half, dtype=F32) / half)
    ang = pos.astype(F32)[:, None] * inv[None, :]
    cos, sin = jnp.cos(ang), jnp.sin(ang)
    cos = jnp.concatenate([cos, cos], axis=-1)
    sin = jnp.concatenate([-sin, sin], axis=-1)
    return jnp.tile(cos, (1, LANES // ATT_HEAD_DIM)), jnp.tile(sin, (1, LANES // ATT_HEAD_DIM))


def _layer_params(l, norm_w, w_in, conv_w, conv_b, dt_bias, a_log, d_skip, ssd_norm_w, q_norm_w,
                  k_norm_w, w_ssd_br, w_att_br, w_out):
    w = w_in[l]
    dt0 = D_SSD + D_XBC
    dt1 = dt0 + SSD_HEADS
    pad = LANES - SSD_HEADS
    lane_i = np.arange(LANES)
    return dict(
        norm_w=norm_w[l][None, :],
        w_main=jnp.concatenate([w[:, :dt0], w[:, dt1:]], axis=1).astype(BF16),
        w_dt=jnp.pad(w[:, dt0:dt1], ((0, 0), (0, pad))).astype(BF16),
        conv_w=conv_w[l], conv_b=conv_b[l][None, :],
        dt_bias=jnp.pad(dt_bias[l].astype(F32), (0, pad))[None, :],
        a_neg=jnp.pad(-jnp.exp(a_log[l].astype(F32)), (0, pad))[None, :],
        d_skip=jnp.repeat(d_skip[l].astype(F32), SSD_HEAD_DIM)[None, :],
        ssd_norm_w=ssd_norm_w[l][None, :],
        head_expand=jnp.asarray(
            lane_i[:, None] == (np.arange(D_SSD) // SSD_HEAD_DIM)[None, :], BF16),
        head_seg=jnp.asarray(
            lane_i[:, None] // ATT_HEAD_DIM == lane_i[None, :] // ATT_HEAD_DIM, BF16),
        q_norm_w=q_norm_w[l], k_norm_w=k_norm_w[l],
        w_ssd_br=w_ssd_br[l].astype(BF16), w_att_br=w_att_br[l].astype(BF16),
        w_out=w_out[l].astype(BF16),
    )


def _prompt_layer(x, p):
    b, s, _ = x.shape
    x2d = x.reshape(b * s, D_MODEL)
    tm = 1024 if (b * s) % 1024 == 0 else s
    u, dt = _inproj(x2d, p["norm_w"], p["w_main"], p["w_dt"], BF16, tm, 1024)
    conv0 = jnp.zeros((b, CONV_WIDTH - 1, D_XBC), F32)
    ssm0 = jnp.zeros((b, SSD_HEADS, SSD_HEAD_DIM, SSD_STATE), F32)
    br_ssd, conv_new, ssm_new = _ssd(u.reshape(b, s, N_U), dt.reshape(b, s, LANES), conv0, ssm0, p, BF16)
    cos, sin = _rope_tables(jnp.arange(s, dtype=jnp.int32))
    qw = jnp.tile(p["q_norm_w"], D_ATT // ATT_HEAD_DIM)[None, :]
    kw = jnp.tile(p["k_norm_w"], D_ATT // ATT_HEAD_DIM)[None, :]
    prev, kvs = [], []
    for gi in range(len(ATT_GROUPS)):
        last = gi == len(ATT_GROUPS) - 1
        o, lse, kv = _attn_prompt(u, b, s, gi, cos, sin, qw, kw, p["head_seg"], tuple(prev) if last else ())
        prev += [o, lse]
        kvs.append(kv)
    tmo = 512 if (b * s) % 512 == 0 else s
    y = _out_proj(o, u, br_ssd.reshape(b * s, D_MODEL), x2d, p["w_att_br"], p["w_out"], tmo)
    return y.reshape(b, s, D_MODEL), conv_new, ssm_new, kvs


def _decode_layer(x, conv_state, ssm_state, caches, p):
    b, s, _ = x.shape
    assert s == 1
    x2d = x.reshape(b, D_MODEL)
    u, dt = _inproj(x2d, p["norm_w"], p["w_main"], p["w_dt"], F32, b, 1024)
    br_ssd, conv_new, ssm_new = _ssd(u.reshape(b, s, N_U), dt.reshape(b, s, LANES),
                                     conv_state, ssm_state, p, F32)
    cos, sin = _rope_tables(PAST_LEN + jnp.arange(s, dtype=jnp.int32))
    nq = D_ATT_QK // LANES
    w24 = jnp.concatenate([jnp.tile(p["q_norm_w"], (nq, LANES // ATT_HEAD_DIM)),
                           jnp.tile(p["k_norm_w"], (nq, LANES // ATT_HEAD_DIM))], axis=0)
    o, kvs = _attn_decode(u, caches, cos, sin, w24, p["head_seg"])
    y = _out_proj(o, u, br_ssd.reshape(b, D_MODEL), x2d, p["w_att_br"], p["w_out"], b)
    return y.reshape(b, s, D_MODEL), conv_new, ssm_new, kvs


def kernel(x_prompt, x_sample, state_conv, state_ssm, cache_kv_w128, cache_kv_w512, cache_kv_w2048,
           norm_w, w_in, conv_w, conv_b, dt_bias, a_log, d_skip, ssd_norm_w, q_norm_w, k_norm_w,
           w_ssd_br, w_att_br, w_out):
    yp, ys = x_prompt, x_sample
    outs_p = [[] for _ in range(5)]
    outs_s = [[] for _ in range(5)]
    for l in range(norm_w.shape[0]):
        p = _layer_params(l, norm_w, w_in, conv_w, conv_b, dt_bias, a_log, d_skip, ssd_norm_w,
                          q_norm_w, k_norm_w, w_ssd_br, w_att_br, w_out)
        yp, c, h, kv = _prompt_layer(yp, p)
        for acc, val in zip(outs_p, [c, h] + kv):
            acc.append(val)
        ys, c, h, kv = _decode_layer(ys, state_conv[l], state_ssm[l],
                                     [cache_kv_w128[l], cache_kv_w512[l], cache_kv_w2048[l]], p)
        for acc, val in zip(outs_s, [c, h] + kv):
            acc.append(val)
    return (yp, ys, *[jnp.stack(a) for a in outs_p], *[jnp.stack(a) for a in outs_s])
```

```python
import functools

import numpy as np
import jax
import jax.numpy as jnp
from jax import lax
from jax.experimental import pallas as pl
from jax.experimental.pallas import tpu as pltpu

F32 = jnp.float32
BF16 = jnp.bfloat16

D_MODEL = 1024
D_SSD = 2048
SSD_HEADS = 32
SSD_HEAD_DIM = 64
SSD_GROUPS = 4
SSD_STATE = 128
CONV_WIDTH = 4
D_BC = SSD_GROUPS * SSD_STATE
D_XBC = D_SSD + 2 * D_BC
ATT_HEAD_DIM = 64
ATT_SLOTS = 8
ATT_GROUPS = ((128, 1), (512, 4), (2048, 16))
D_ATT = ATT_SLOTS * ATT_HEAD_DIM
D_ATT_QK = 3 * D_ATT
ATT_SCALE = ATT_HEAD_DIM ** -0.5
ROPE_THETA = 10000.0
EPS = 1e-6
PAST_LEN = 16384

N_U = 12288
U_Z, U_X, U_B, U_C, U_Q, U_K, U_V, U_ZATT, U_GSSD, U_GATT = (
    0, 2048, 4096, 4608, 5120, 6656, 8192, 9728, 10240, 11264)

LANES = 128
SUBLANES = 8
SSD_CHUNK = 128
ATT_BLOCK = 128
VMEM_LIMIT = 56 * 1024 * 1024


def _split_dot(a, m, terms, left=False):
    out = None
    r = a
    for t in range(terms):
        p = r.astype(BF16)
        d = (jnp.dot(m, p, preferred_element_type=F32) if left
             else jnp.dot(p, m, preferred_element_type=F32))
        out = d if out is None else out + d
        if t + 1 < terms:
            r = r - p.astype(F32)
    return out


def _sigmoid(x):
    return 1.0 / (1.0 + jnp.exp(-x))


def _silu(x):
    return x * _sigmoid(x)


def _softplus(x):
    return jnp.maximum(x, 0.0) + jnp.log1p(jnp.exp(-jnp.abs(x)))


def _inproj_kernel(x_ref, nw_ref, w_ref, wdt_ref, u_ref, dt_ref, hn_ref):
    @pl.when(pl.program_id(1) == 0)
    def _():
        x = x_ref[...]
        ms = jnp.mean(x * x, axis=-1, keepdims=True)
        hn = (x * lax.rsqrt(ms + EPS) * nw_ref[...]).astype(BF16)
        hn_ref[...] = hn
        dt_ref[...] = jnp.dot(hn, wdt_ref[...], preferred_element_type=F32)

    u_ref[...] = jnp.dot(hn_ref[...], w_ref[...], preferred_element_type=F32).astype(u_ref.dtype)


def _inproj(x2d, norm_w, w_main, w_dt, out_dtype, tm, tn):
    m = x2d.shape[0]
    return pl.pallas_call(
        _inproj_kernel,
        grid=(m // tm, N_U // tn),
        in_specs=[
            pl.BlockSpec((tm, D_MODEL), lambda i, j: (i, 0)),
            pl.BlockSpec((1, D_MODEL), lambda i, j: (0, 0)),
            pl.BlockSpec((D_MODEL, tn), lambda i, j: (0, j)),
            pl.BlockSpec((D_MODEL, LANES), lambda i, j: (0, 0)),
        ],
        out_specs=[
            pl.BlockSpec((tm, tn), lambda i, j: (i, j)),
            pl.BlockSpec((tm, LANES), lambda i, j: (i, 0)),
        ],
        out_shape=[
            jax.ShapeDtypeStruct((m, N_U), out_dtype),
            jax.ShapeDtypeStruct((m, LANES), F32),
        ],
        scratch_shapes=[pltpu.VMEM((tm, D_MODEL), BF16)],
        compiler_params=pltpu.CompilerParams(
            dimension_semantics=("parallel", "arbitrary"), vmem_limit_bytes=VMEM_LIMIT),
        name="inproj",
    )(x2d, norm_w, w_main, w_dt)


def _ssd_kernel(z_ref, x_ref, bm_ref, cm_ref, dt_ref, cs_ref, h0_ref, cw_ref, cb_ref, dtb_ref,
                a_ref, dsk_ref, nw_ref, e_ref, wbr_ref, y_ref, cso_ref, ho_ref,
                xin_ref, dtin_ref, ht_ref, yd_ref, *pad_refs, q, t, nc):
    c = pl.program_id(1)
    tail = CONV_WIDTH - 1
    base = SUBLANES
    hp = D_SSD
    gw = D_SSD // SSD_GROUPS
    e_per_g = SSD_HEADS // SSD_GROUPS

    @pl.when(c == 0)
    def _():
        xin_ref[0:base, :] = jnp.zeros((base, D_XBC), F32)
        xin_ref[base - tail:base, :] = cs_ref[0]
        ht_ref[...] = h0_ref[0].reshape(hp, SSD_STATE).T

    if t < q:
        xin_ref[base:base + q, :] = jnp.zeros((q, D_XBC), F32)
        dtin_ref[...] = jnp.zeros((q, LANES), F32)
        (zin_ref,) = pad_refs
        zin_ref[...] = jnp.zeros((q, D_SSD), F32)
        zin_ref[0:t, :] = z_ref[0].astype(F32)
    xin_ref[base:base + t, 0:D_SSD] = x_ref[0].astype(F32)
    xin_ref[base:base + t, D_SSD:D_SSD + D_BC] = bm_ref[0].astype(F32)
    xin_ref[base:base + t, D_SSD + D_BC:D_XBC] = cm_ref[0].astype(F32)
    dtin_ref[0:t, :] = dt_ref[0]

    def conv(lo, hi):
        acc = cb_ref[:, lo:hi]
        for j in range(CONV_WIDTH):
            acc = acc + cw_ref[j:j + 1, lo:hi] * xin_ref[base - tail + j:base - tail + j + q, lo:hi]
        return _silu(acc)

    dt = _softplus(dtin_ref[...] + dtb_ref[...])
    if t < q:
        rows = lax.broadcasted_iota(jnp.int32, (q, LANES), 0)
        dt = jnp.where(rows < t, dt, 0.0)
    a = dt * a_ref[...]
    ri = lax.broadcasted_iota(jnp.int32, (q, q), 0)
    ci = lax.broadcasted_iota(jnp.int32, (q, q), 1)
    causal = ri >= ci
    tril = jnp.where(causal, 1.0, 0.0).astype(BF16)
    acum = _split_dot(a, tril, 3, left=True)
    acum_t = acum.T
    a_last = acum[q - 1:q, :]
    e_mat = e_ref[...]
    dt_e = _split_dot(dt, e_mat, 3)
    ea_e = _split_dot(jnp.exp(acum), e_mat, 3)
    ds_e = _split_dot(jnp.exp(a_last - acum), e_mat, 3)

    bmat = conv(D_SSD, D_SSD + D_BC)
    cmat = conv(D_SSD + D_BC, D_XBC).astype(BF16)
    bt = bmat.T.astype(BF16)

    yn_parts = []
    for g in range(SSD_GROUPS):
        gs = slice(g * gw, (g + 1) * gw)
        xs = conv(g * gw, (g + 1) * gw)
        xdt = xs * dt_e[:, gs]
        xdt_b = xdt.astype(BF16)
        cg = cmat[:, g * SSD_STATE:(g + 1) * SSD_STATE]
        bgt = bt[g * SSD_STATE:(g + 1) * SSD_STATE, :]
        cb = jnp.dot(cg, bgt, preferred_element_type=F32)
        ht_g = ht_ref[:, gs]
        y_off = jnp.dot(cg, ht_g.astype(BF16), preferred_element_type=F32)
        for e in range(e_per_g):
            h = g * e_per_g + e
            seg = acum[:, h:h + 1] - acum_t[h:h + 1, :]
            lmat = jnp.exp(jnp.where(causal, seg, -jnp.inf))
            mh = (cb * lmat).astype(BF16)
            hs = slice(e * SSD_HEAD_DIM, (e + 1) * SSD_HEAD_DIM)
            yd_ref[:, h * SSD_HEAD_DIM:(h + 1) * SSD_HEAD_DIM] = jnp.dot(
                mh, xdt_b[:, hs], preferred_element_type=F32)
        y = yd_ref[:, gs] + y_off * ea_e[:, gs] + dsk_ref[:, gs] * xs
        xw = (xdt * ds_e[:, gs]).astype(BF16)
        ht_ref[:, gs] = ht_g * ea_e[q - 1:q, gs] + jnp.dot(bgt, xw, preferred_element_type=F32)
        zg = z_ref[0][:, gs].astype(F32) if t == q else zin_ref[:, gs]
        yg = y * _silu(zg)
        ms = jnp.mean(yg * yg, axis=-1, keepdims=True)
        yn_parts.append((yg * lax.rsqrt(ms + EPS) * nw_ref[:, gs]).astype(BF16))
    yn = jnp.concatenate(yn_parts, axis=-1)
    br = jnp.dot(yn, wbr_ref[...], preferred_element_type=F32)
    y_ref[0] = br[0:t, :].astype(y_ref.dtype)

    @pl.when(c == nc - 1)
    def _():
        cso_ref[0] = xin_ref[base + t - tail:base + t, :]
        ho_ref[0] = ht_ref[...].T.reshape(SSD_HEADS, SSD_HEAD_DIM, SSD_STATE)

    if t == q:
        xin_ref[base - tail:base, :] = xin_ref[base + q - tail:base + q, :]


def _const_spec(shape):
    return pl.BlockSpec(shape, lambda *_: (0,) * len(shape))


def _ssd(u3, dt3, conv_state, ssm_state, p, out_dtype):
    b, s, _ = u3.shape
    q = SSD_CHUNK
    t = min(q, s)
    nc = s // t
    assert s == nc * t and (t == q or nc == 1)

    def ucol(width, off):
        return pl.BlockSpec((1, t, width), lambda bi, ci: (bi, ci, off // width))

    scratch = [
        pltpu.VMEM((SUBLANES + q, D_XBC), F32),
        pltpu.VMEM((q, LANES), F32),
        pltpu.VMEM((SSD_STATE, D_SSD), F32),
        pltpu.VMEM((q, D_SSD), F32),
    ]
    if t < q:
        scratch.append(pltpu.VMEM((q, D_SSD), F32))
    return pl.pallas_call(
        functools.partial(_ssd_kernel, q=q, t=t, nc=nc),
        grid=(b, nc),
        in_specs=[
            ucol(D_SSD, U_Z), ucol(D_SSD, U_X), ucol(D_BC, U_B), ucol(D_BC, U_C),
            pl.BlockSpec((1, t, LANES), lambda bi, ci: (bi, ci, 0)),
            pl.BlockSpec((1, CONV_WIDTH - 1, D_XBC), lambda bi, ci: (bi, 0, 0)),
            pl.BlockSpec((1, SSD_HEADS, SSD_HEAD_DIM, SSD_STATE), lambda bi, ci: (bi, 0, 0, 0)),
            _const_spec((CONV_WIDTH, D_XBC)), _const_spec((1, D_XBC)),
            _const_spec((1, LANES)), _const_spec((1, LANES)),
            _const_spec((1, D_SSD)), _const_spec((1, D_SSD)),
            _const_spec((LANES, D_SSD)), _const_spec((D_SSD, D_MODEL)),
        ],
        out_specs=[
            pl.BlockSpec((1, t, D_MODEL), lambda bi, ci: (bi, ci, 0)),
            pl.BlockSpec((1, CONV_WIDTH - 1, D_XBC), lambda bi, ci: (bi, 0, 0)),
            pl.BlockSpec((1, SSD_HEADS, SSD_HEAD_DIM, SSD_STATE), lambda bi, ci: (bi, 0, 0, 0)),
        ],
        out_shape=[
            jax.ShapeDtypeStruct((b, s, D_MODEL), out_dtype),
            jax.ShapeDtypeStruct((b, CONV_WIDTH - 1, D_XBC), F32),
            jax.ShapeDtypeStruct((b, SSD_HEADS, SSD_HEAD_DIM, SSD_STATE), F32),
        ],
        scratch_shapes=scratch,
        compiler_params=pltpu.CompilerParams(
            dimension_semantics=("parallel", "arbitrary"), vmem_limit_bytes=VMEM_LIMIT),
        name="ssd",
    )(u3, u3, u3, u3, dt3, conv_state, ssm_state, p["conv_w"], p["conv_b"], p["dt_bias"], p["a_neg"],
      p["d_skip"], p["ssd_norm_w"], p["head_expand"], p["w_ssd_br"])


def _norm_rope(x, w, cos, sin, seg):
    width = x.shape[-1]
    sq = x * x
    ss = jnp.concatenate(
        [_split_dot(sq[:, c * LANES:(c + 1) * LANES], seg, 2) for c in range(width // LANES)], axis=-1)
    xn = x * lax.rsqrt(ss * (1.0 / ATT_HEAD_DIM) + EPS) * w
    half = ATT_HEAD_DIM // 2
    ahead = pltpu.roll(xn, width - half, axis=1)
    behind = pltpu.roll(xn, half, axis=1)
    lane = lax.broadcasted_iota(jnp.int32, x.shape, 1)
    partner = jnp.where(lane % ATT_HEAD_DIM < half, ahead, behind)
    return xn * cos + partner * sin


def _attn_prompt_kernel(*refs, merge, nb):
    q_ref, k_ref, v_ref, cos_ref, sin_ref, qw_ref, kw_ref, seg_ref = refs[:8]
    refs = refs[8:]
    if merge:
        o0_ref, l0_ref, o1_ref, l1_ref = refs[:4]
        refs = refs[4:]
    o_ref, l_ref, kv_ref, kp_ref, vp_ref, ob_ref = refs
    n = pl.program_id(2)
    blk = ATT_BLOCK
    hd = ATT_HEAD_DIM

    @pl.when(n == 0)
    def _():
        kp_ref[...] = jnp.zeros(kp_ref.shape, BF16)
        vp_ref[...] = jnp.zeros(vp_ref.shape, BF16)

    reps = D_ATT // LANES
    cos = jnp.concatenate([cos_ref[...]] * reps, axis=-1)
    sin = jnp.concatenate([sin_ref[...]] * reps, axis=-1)
    seg = seg_ref[...]
    qn = _norm_rope(q_ref[0].astype(F32), qw_ref[...], cos, sin, seg)
    kn = _norm_rope(k_ref[0].astype(F32), kw_ref[...], cos, sin, seg)
    v = v_ref[0]
    qb = (qn * ATT_SCALE).astype(BF16)
    kb = kn.astype(BF16)
    kp = kp_ref[...]
    vp = vp_ref[...]

    ri = lax.broadcasted_iota(jnp.int32, (blk, blk), 0)
    ci = lax.broadcasted_iota(jnp.int32, (blk, blk), 1)
    keep_prev = jnp.logical_and(ci >= ri, n > 0)
    keep_cur = ci <= ri
    nt = (((1,), (1,)), ((), ()))
    for h in range(ATT_SLOTS):
        hs = slice(h * hd, (h + 1) * hd)
        s_p = lax.dot_general(qb[:, hs], kp[:, hs], nt, preferred_element_type=F32)
        s_c = lax.dot_general(qb[:, hs], kb[:, hs], nt, preferred_element_type=F32)
        s_p = jnp.where(keep_prev, s_p, -jnp.inf)
        s_c = jnp.where(keep_cur, s_c, -jnp.inf)
        m = jnp.maximum(jnp.max(s_p, axis=-1, keepdims=True), jnp.max(s_c, axis=-1, keepdims=True))
        p_p = jnp.exp(s_p - m)
        p_c = jnp.exp(s_c - m)
        den = jnp.sum(p_p, axis=-1, keepdims=True) + jnp.sum(p_c, axis=-1, keepdims=True)
        o_h = (jnp.dot(p_p.astype(BF16), vp[:, hs], preferred_element_type=F32)
               + jnp.dot(p_c.astype(BF16), v[:, hs], preferred_element_type=F32)) / den
        lse = m + jnp.log(den)
        if merge:
            ls = slice(h * 16, h * 16 + 1)
            la, lb = l0_ref[0][:, ls], l1_ref[0][:, ls]
            top = jnp.maximum(jnp.maximum(la, lb), lse)
            wa, wb, wc = jnp.exp(la - top), jnp.exp(lb - top), jnp.exp(lse - top)
            o_h = (wa * o0_ref[0][:, hs].astype(F32) + wb * o1_ref[0][:, hs].astype(F32)
                   + wc * o_h) / (wa + wb + wc)
        ob_ref[:, hs] = o_h
        l_ref[0, :, h * 16:(h + 1) * 16] = jnp.broadcast_to(lse, (blk, 16))
    o_ref[0] = ob_ref[...].astype(o_ref.dtype)

    @pl.when(n == nb - 1)
    def _():
        kv_ref[0, :, 0:D_ATT] = kn
        kv_ref[0, :, D_ATT:2 * D_ATT] = v.astype(F32)

    kp_ref[...] = kb
    vp_ref[...] = v


def _attn_prompt(u2d, b, s, gi, cos, sin, qw, kw, seg, prev):
    win, dil = ATT_GROUPS[gi]
    rows = s // dil
    nb = rows // ATT_BLOCK
    assert win // dil == ATT_BLOCK and rows == nb * ATT_BLOCK and s >= win
    ncol = N_U // D_ATT

    def ucol(off):
        return pl.BlockSpec((1, ATT_BLOCK, D_ATT),
                            lambda bi, r, n: (bi, n, r * ncol + off // D_ATT + gi))

    tok = lambda width: pl.BlockSpec((1, ATT_BLOCK, width), lambda bi, r, n: (bi, n, r))
    tab = pl.BlockSpec((ATT_BLOCK, LANES), lambda bi, r, n: (n, r))
    u3 = u2d.reshape(b, rows, dil * N_U)
    prev_specs, prev_args = [], []
    for i, arr in enumerate(prev):
        width = D_ATT if i % 2 == 0 else LANES
        prev_specs.append(tok(width))
        prev_args.append(arr.reshape(b, rows, dil * width))
    o, lse, kv = pl.pallas_call(
        functools.partial(_attn_prompt_kernel, merge=bool(prev), nb=nb),
        grid=(b, dil, nb),
        in_specs=[ucol(U_Q), ucol(U_K), ucol(U_V), tab, tab,
                  _const_spec((1, D_ATT)), _const_spec((1, D_ATT)), _const_spec((LANES, LANES))]
        + prev_specs,
        out_specs=[tok(D_ATT), tok(LANES),
                   pl.BlockSpec((1, ATT_BLOCK, 2 * D_ATT), lambda bi, r, n: (bi, 0, r))],
        out_shape=[
            jax.ShapeDtypeStruct((b, rows, dil * D_ATT), BF16),
            jax.ShapeDtypeStruct((b, rows, dil * LANES), F32),
            jax.ShapeDtypeStruct((b, ATT_BLOCK, dil * 2 * D_ATT), F32),
        ],
        scratch_shapes=[pltpu.VMEM((ATT_BLOCK, D_ATT), BF16), pltpu.VMEM((ATT_BLOCK, D_ATT), BF16),
                        pltpu.VMEM((ATT_BLOCK, D_ATT), F32)],
        compiler_params=pltpu.CompilerParams(
            dimension_semantics=("parallel", "parallel", "arbitrary"), vmem_limit_bytes=VMEM_LIMIT),
        name=f"attn_prompt_g{gi}",
    )(u3, u3, u3, cos.reshape(rows, dil * LANES), sin.reshape(rows, dil * LANES), qw, kw, seg, *prev_args)
    return (o.reshape(b * s, D_ATT), lse.reshape(b * s, LANES),
            kv.reshape(b, win, 2, ATT_SLOTS, ATT_HEAD_DIM))


def _attn_decode_kernel(x_ref, c0_ref, c1_ref, c2_ref, cos_ref, sin_ref, qw_ref, kw_ref,
                        o_ref, n0_ref, n1_ref, n2_ref):
    caches = (c0_ref, c1_ref, c2_ref)
    news = (n0_ref, n1_ref, n2_ref)
    ng = len(ATT_GROUPS)
    cos, sin = cos_ref[...], sin_ref[...]
    half = ATT_HEAD_DIM // 2

    def norm_rope(x, w):
        ms = jnp.mean(x * x, axis=0, keepdims=True)
        xn = x * lax.rsqrt(ms + EPS) * w
        partner = jnp.concatenate([xn[half:], xn[:half]], axis=0)
        return xn * cos + partner * sin

    outs_g, lses = [], []
    for g, (win, dil) in enumerate(ATT_GROUPS):
        qc = norm_rope(x_ref[0, 0, g], qw_ref[...]) * ATT_SCALE
        kc = norm_rope(x_ref[0, 0, ng + g], kw_ref[...])
        vc = x_ref[0, 0, 2 * ng + g]
        kt = caches[g][0, 0, 0]
        vt = caches[g][0, 1, 0]
        lane = lax.broadcasted_iota(jnp.int32, (1, win), 1)
        sc = jnp.sum(kt * qc, axis=0, keepdims=True)
        sc = jnp.where(lane % dil == 0, sc, -jnp.inf)
        sc_new = jnp.sum(kc * qc, axis=0, keepdims=True)
        m = jnp.maximum(jnp.max(sc, axis=1, keepdims=True), sc_new)
        p = jnp.exp(sc - m)
        p_new = jnp.exp(sc_new - m)
        den = jnp.sum(p, axis=1, keepdims=True) + p_new
        outs_g.append((jnp.sum(vt * p, axis=1, keepdims=True) + p_new * vc) / den)
        lses.append(m + jnp.log(den))
        last = lane == win - 1
        news[g][0, 0, 0] = jnp.where(last, kc, pltpu.roll(kt, win - 1, axis=1))
        news[g][0, 1, 0] = jnp.where(last, vc, pltpu.roll(vt, win - 1, axis=1))

    top = jnp.maximum(jnp.maximum(lses[0], lses[1]), lses[2])
    ws = [jnp.exp(l - top) for l in lses]
    o_ref[0, 0] = (ws[0] * outs_g[0] + ws[1] * outs_g[1] + ws[2] * outs_g[2]) / (ws[0] + ws[1] + ws[2])


def _attn_decode(us, caches, cos, sin, qw, kw):
    b = us.shape[0]
    ng = len(ATT_GROUPS)
    hd = ATT_HEAD_DIM
    x = us[:, U_Q:U_Q + 3 * D_ATT_QK].reshape(b, 3, ng, ATT_SLOTS, hd)
    x = x.transpose(0, 3, 1, 2, 4).reshape(b, ATT_SLOTS, 3 * ng, hd, 1)
    cts = [c.transpose(0, 2, 3, 4, 1) for c in caches]
    cache_specs = []
    for c, (win, dil) in zip(cts, ATT_GROUPS):
        assert c.shape == (b, 2, ATT_SLOTS, hd, win) and win // dil == ATT_BLOCK
        cache_specs.append(pl.BlockSpec((1, 2, 1, hd, win), lambda bi, h: (bi, 0, h, 0, 0)))
    col = _const_spec((hd, 1))
    res = pl.pallas_call(
        _attn_decode_kernel,
        grid=(b, ATT_SLOTS),
        in_specs=[pl.BlockSpec((1, 1, 3 * ng, hd, 1), lambda bi, h: (bi, h, 0, 0, 0))] + cache_specs
        + [col, col, col, col],
        out_specs=[pl.BlockSpec((1, 1, hd, 1), lambda bi, h: (bi, h, 0, 0))] + cache_specs,
        out_shape=[jax.ShapeDtypeStruct((b, ATT_SLOTS, hd, 1), F32)]
        + [jax.ShapeDtypeStruct(c.shape, F32) for c in cts],
        compiler_params=pltpu.CompilerParams(
            dimension_semantics=("parallel", "parallel"), vmem_limit_bytes=VMEM_LIMIT),
        name="attn_decode",
    )(x, *cts, cos.reshape(hd, 1), sin.reshape(hd, 1), qw.reshape(hd, 1), kw.reshape(hd, 1))
    o = res[0].reshape(b, D_ATT)
    kvs = [r.transpose(0, 4, 1, 2, 3) for r in res[1:]]
    return o, kvs


def _out_kernel(o_ref, z_ref, br_ref, gs_ref, ga_ref, x_ref, wa_ref, wo_ref, y_ref):
    o = o_ref[...].astype(F32) * _silu(z_ref[...].astype(F32))
    br_att = jnp.dot(o.astype(BF16), wa_ref[...], preferred_element_type=F32)
    mix = (_sigmoid(gs_ref[...].astype(F32)) * br_ref[...].astype(F32)
           + _sigmoid(ga_ref[...].astype(F32)) * br_att)
    y_ref[...] = x_ref[...] + jnp.dot(mix.astype(BF16), wo_ref[...], preferred_element_type=F32)


def _out_proj(o, u2d, br_ssd, x2d, w_att_br, w_out, tm):
    m = x2d.shape[0]
    row = lambda width, off=0: pl.BlockSpec((tm, width), lambda i: (i, off // width))
    return pl.pallas_call(
        _out_kernel,
        grid=(m // tm,),
        in_specs=[row(D_ATT), row(D_ATT, U_ZATT), row(D_MODEL), row(D_MODEL, U_GSSD),
                  row(D_MODEL, U_GATT), row(D_MODEL),
                  _const_spec((D_ATT, D_MODEL)), _const_spec((D_MODEL, D_MODEL))],
        out_specs=row(D_MODEL),
        out_shape=jax.ShapeDtypeStruct((m, D_MODEL), F32),
        compiler_params=pltpu.CompilerParams(
            dimension_semantics=("parallel",), vmem_limit_bytes=VMEM_LIMIT),
        name="out_proj",
    )(o, u2d, br_ssd, u2d, u2d, x2d, w_att_br, w_out)


def _rope_tables(pos):
    half = ATT_HEAD_DIM // 2
    inv = ROPE_THETA ** (-jnp.arange(half, dtype=F32) / half)
    ang = pos.astype(F32)[:, None] * inv[None, :]
    cos, sin = jnp.cos(ang), jnp.sin(ang)
    cos = jnp.concatenate([cos, cos], axis=-1)
    sin = jnp.concatenate([-sin, sin], axis=-1)
    return jnp.tile(cos, (1, LANES // ATT_HEAD_DIM)), jnp.tile(sin, (1, LANES // ATT_HEAD_DIM))


def _layer_params(l, norm_w, w_in, conv_w, conv_b, dt_bias, a_log, d_skip, ssd_norm_w, q_norm_w,
                  k_norm_w, w_ssd_br, w_att_br, w_out):
    w = w_in[l]
    dt0 = D_SSD + D_XBC
    dt1 = dt0 + SSD_HEADS
    pad = LANES - SSD_HEADS
    lane_i = np.arange(LANES)
    return dict(
        norm_w=norm_w[l][None, :],
        w_main=jnp.concatenate([w[:, :dt0], w[:, dt1:]], axis=1).astype(BF16),
        w_dt=jnp.pad(w[:, dt0:dt1], ((0, 0), (0, pad))).astype(BF16),
        conv_w=conv_w[l], conv_b=conv_b[l][None, :],
        dt_bias=jnp.pad(dt_bias[l].astype(F32), (0, pad))[None, :],
        a_neg=jnp.pad(-jnp.exp(a_log[l].astype(F32)), (0, pad))[None, :],
        d_skip=jnp.repeat(d_skip[l].astype(F32), SSD_HEAD_DIM)[None, :],
        ssd_norm_w=ssd_norm_w[l][None, :],
        head_expand=jnp.asarray(
            lane_i[:, None] == (np.arange(D_SSD) // SSD_HEAD_DIM)[None, :], BF16),
        head_seg=jnp.asarray(
            lane_i[:, None] // ATT_HEAD_DIM == lane_i[None, :] // ATT_HEAD_DIM, BF16),
        q_norm_w=q_norm_w[l], k_norm_w=k_norm_w[l],
        w_ssd_br=w_ssd_br[l].astype(BF16), w_att_br=w_att_br[l].astype(BF16),
        w_out=w_out[l].astype(BF16),
    )


def _prompt_layer(x, p):
    b, s, _ = x.shape
    x2d = x.reshape(b * s, D_MODEL)
    tm = 1024 if (b * s) % 1024 == 0 else s
    u, dt = _inproj(x2d, p["norm_w"], p["w_main"], p["w_dt"], BF16, tm, 1024)
    conv0 = jnp.zeros((b, CONV_WIDTH - 1, D_XBC), F32)
    ssm0 = jnp.zeros((b, SSD_HEADS, SSD_HEAD_DIM, SSD_STATE), F32)
    br_ssd, conv_new, ssm_new = _ssd(u.reshape(b, s, N_U), dt.reshape(b, s, LANES), conv0, ssm0, p, BF16)
    cos, sin = _rope_tables(jnp.arange(s, dtype=jnp.int32))
    qw = jnp.tile(p["q_norm_w"], D_ATT // ATT_HEAD_DIM)[None, :]
    kw = jnp.tile(p["k_norm_w"], D_ATT // ATT_HEAD_DIM)[None, :]
    prev, kvs = [], []
    for gi in range(len(ATT_GROUPS)):
        last = gi == len(ATT_GROUPS) - 1
        o, lse, kv = _attn_prompt(u, b, s, gi, cos, sin, qw, kw, p["head_seg"], tuple(prev) if last else ())
        prev += [o, lse]
        kvs.append(kv)
    tmo = 512 if (b * s) % 512 == 0 else s
    y = _out_proj(o, u, br_ssd.reshape(b * s, D_MODEL), x2d, p["w_att_br"], p["w_out"], tmo)
    return y.reshape(b, s, D_MODEL), conv_new, ssm_new, kvs


def _decode_layer(x, conv_state, ssm_state, caches, p):
    b, s, _ = x.shape
    assert s == 1
    x2d = x.reshape(b, D_MODEL)
    u, dt = _inproj(x2d, p["norm_w"], p["w_main"], p["w_dt"], F32, b, 1024)
    br_ssd, conv_new, ssm_new = _ssd(u.reshape(b, s, N_U), dt.reshape(b, s, LANES),
                                     conv_state, ssm_state, p, F32)
    cos, sin = _rope_tables(PAST_LEN + jnp.arange(s, dtype=jnp.int32))
    o, kvs = _attn_decode(u, caches, cos[:, :ATT_HEAD_DIM], sin[:, :ATT_HEAD_DIM],
                          p["q_norm_w"][None, :], p["k_norm_w"][None, :])
    y = _out_proj(o, u, br_ssd.reshape(b, D_MODEL), x2d, p["w_att_br"], p["w_out"], b)
    return y.reshape(b, s, D_MODEL), conv_new, ssm_new, kvs


def kernel(x_prompt, x_sample, state_conv, state_ssm, cache_kv_w128, cache_kv_w512, cache_kv_w2048,
           norm_w, w_in, conv_w, conv_b, dt_bias, a_log, d_skip, ssd_norm_w, q_norm_w, k_norm_w,
           w_ssd_br, w_att_br, w_out):
    yp, ys = x_prompt, x_sample
    outs_p = [[] for _ in range(5)]
    outs_s = [[] for _ in range(5)]
    for l in range(norm_w.shape[0]):
        p = _layer_params(l, norm_w, w_in, conv_w, conv_b, dt_bias, a_log, d_skip, ssd_norm_w,
                          q_norm_w, k_norm_w, w_ssd_br, w_att_br, w_out)
        yp, c, h, kv = _prompt_layer(yp, p)
        for acc, val in zip(outs_p, [c, h] + kv):
            acc.append(val)
        ys, c, h, kv = _decode_layer(ys, state_conv[l], state_ssm[l],
                                     [cache_kv_w128[l], cache_kv_w512[l], cache_kv_w2048[l]], p)
        for acc, val in zip(outs_s, [c, h] + kv):
            acc.append(val)
    return (yp, ys, *[jnp.stack(a) for a in outs_p], *[jnp.stack(a) for a in outs_s])
```

```python
import functools

import numpy as np
import jax
import jax.numpy as jnp
from jax import lax
from jax.experimental import pallas as pl
from jax.experimental.pallas import tpu as pltpu

F32 = jnp.float32
BF16 = jnp.bfloat16

D_MODEL = 1024
D_SSD = 2048
SSD_HEADS = 32
SSD_HEAD_DIM = 64
SSD_GROUPS = 4
SSD_STATE = 128
CONV_WIDTH = 4
D_BC = SSD_GROUPS * SSD_STATE
D_XBC = D_SSD + 2 * D_BC
ATT_HEAD_DIM = 64
ATT_SLOTS = 8
ATT_GROUPS = ((128, 1), (512, 4), (2048, 16))
N_ATT_GROUPS = len(ATT_GROUPS)
D_ATT = ATT_SLOTS * ATT_HEAD_DIM
D_ATT_QK = N_ATT_GROUPS * D_ATT
ATT_SCALE = ATT_HEAD_DIM ** -0.5
ROPE_THETA = 10000.0
EPS = 1e-6
PAST_LEN = 16384

U_Z, U_X, U_B, U_C, U_GSSD, U_GATT, U_ZATT = 0, 2048, 4096, 4608, 5120, 6144, 7168
U_ATT = 7680
U_TILE = 3 * D_ATT
N_U = U_ATT + N_ATT_GROUPS * U_TILE
N_PLAIN_TILES = U_ATT // U_TILE

LANES = 128
SUBLANES = 8
SSD_CHUNK = 128
ATT_BLOCK = 128
ATT_SB = 2048
IN_TM = 1024
VMEM_LIMIT = 56 * 1024 * 1024


def _split_dot(a, m, terms, left=False):
    out = None
    r = a
    for t in range(terms):
        p = r.astype(BF16)
        d = (jnp.dot(m, p, preferred_element_type=F32) if left
             else jnp.dot(p, m, preferred_element_type=F32))
        out = d if out is None else out + d
        if t + 1 < terms:
            r = r - p.astype(F32)
    return out


def _sigmoid(x):
    return 1.0 / (1.0 + jnp.exp(-x))


def _silu(x):
    return x * _sigmoid(x)


def _softplus(x):
    return jnp.maximum(x, 0.0) + jnp.log1p(jnp.exp(-jnp.abs(x)))


def _const_spec(shape):
    return pl.BlockSpec(shape, lambda *_: (0,) * len(shape))


def _residue_pieces(win, dil, tm):
    span = min(win, tm)
    per = span // dil
    return [(w * span + r * per, w * span + r, per)
            for w in range(max(tm // win, 1)) for r in range(dil)]


def _inproj_kernel(x_ref, nw_ref, w_ref, wdt_ref, cos_ref, sin_ref, qkw_ref, seg_ref,
                   u_ref, dt_ref, hn_ref, *perm_refs, tm, permute):
    j = pl.program_id(1)

    @pl.when(j == 0)
    def _():
        x = x_ref[...]
        ms = jnp.mean(x * x, axis=-1, keepdims=True)
        hn = x * lax.rsqrt(ms + EPS) * nw_ref[...]
        hn_ref[0] = hn.astype(BF16)
        dt_ref[...] = jnp.dot(hn_ref[0], wdt_ref[...], preferred_element_type=F32)
        if permute:
            (hnf_ref,) = perm_refs
            for c in range(D_MODEL // LANES):
                hnf_ref[c] = hn[:, c * LANES:(c + 1) * LANES]
            for g, (win, dil) in enumerate(ATT_GROUPS):
                if dil == 1:
                    continue
                for dst, src, cnt in _residue_pieces(win, dil, tm):
                    for c in range(D_MODEL // LANES):
                        hn_ref[g, dst:dst + cnt, c * LANES:(c + 1) * LANES] = (
                            hnf_ref[c, pl.ds(src, cnt, stride=dil), :].astype(BF16))

    @pl.when(j < N_PLAIN_TILES)
    def _():
        u_ref[...] = jnp.dot(hn_ref[0], w_ref[...], preferred_element_type=F32).astype(u_ref.dtype)

    half = ATT_HEAD_DIM // 2
    for g, (_, dil) in enumerate(ATT_GROUPS):
        @pl.when(j == N_PLAIN_TILES + g)
        def _(g=g, dil=dil):
            src = g if (permute and dil > 1) else 0
            acc = jnp.dot(hn_ref[src], w_ref[...], preferred_element_type=F32)
            cos, sin, seg = cos_ref[0], sin_ref[0], seg_ref[...]
            lane = lax.broadcasted_iota(jnp.int32, (tm, LANES), 1)
            first = lane % ATT_HEAD_DIM < half
            for c in range(2 * D_ATT // LANES):
                cs = slice(c * LANES, (c + 1) * LANES)
                xs = acc[:, cs]
                ss = _split_dot(xs * xs, seg, 2)
                xn = xs * lax.rsqrt(ss * (1.0 / ATT_HEAD_DIM) + EPS) * qkw_ref[:, cs]
                partner = jnp.where(first, pltpu.roll(xn, LANES - half, axis=1),
                                    pltpu.roll(xn, half, axis=1))
                u_ref[:, cs] = (xn * cos + partner * sin).astype(u_ref.dtype)
            u_ref[:, 2 * D_ATT:] = acc[:, 2 * D_ATT:].astype(u_ref.dtype)


def _inproj(x2d, p, cos3, sin3, out_dtype, tm, permute):
    m = x2d.shape[0]
    period = cos3.shape[1] // tm
    tab = pl.BlockSpec((1, tm, LANES),
                       lambda i, j: (jnp.clip(j - N_PLAIN_TILES, 0, N_ATT_GROUPS - 1), i % period, 0))
    scratch = [pltpu.VMEM((N_ATT_GROUPS if permute else 1, tm, D_MODEL), BF16)]
    if permute:
        scratch.append(pltpu.VMEM((D_MODEL // LANES, tm, LANES), F32))
    return pl.pallas_call(
        functools.partial(_inproj_kernel, tm=tm, permute=permute),
        grid=(m // tm, N_U // U_TILE),
        in_specs=[
            pl.BlockSpec((tm, D_MODEL), lambda i, j: (i, 0)),
            _const_spec((1, D_MODEL)),
            pl.BlockSpec((D_MODEL, U_TILE), lambda i, j: (0, j)),
            _const_spec((D_MODEL, LANES)),
            tab, tab,
            _const_spec((1, 2 * D_ATT)),
            _const_spec((LANES, LANES)),
        ],
        out_specs=[
            pl.BlockSpec((tm, U_TILE), lambda i, j: (i, j)),
            pl.BlockSpec((tm, LANES), lambda i, j: (i, 0)),
        ],
        out_shape=[
            jax.ShapeDtypeStruct((m, N_U), out_dtype),
            jax.ShapeDtypeStruct((m, LANES), F32),
        ],
        scratch_shapes=scratch,
        compiler_params=pltpu.CompilerParams(
            dimension_semantics=("parallel", "arbitrary"), vmem_limit_bytes=VMEM_LIMIT),
        name="inproj",
    )(x2d, p["norm_w"], p["w_main"], p["w_dt"], cos3, sin3, p["qk_norm_w"], p["head_seg"])


def _ssd_kernel(z_ref, x_ref, bm_ref, cm_ref, dt_ref, cs_ref, h0_ref, cw_ref, cb_ref, dtb_ref,
                a_ref, dsk_ref, nw_ref, e_ref, wbr_ref, y_ref, cso_ref, ho_ref,
                xin_ref, dtin_ref, ht_ref, yd_ref, *pad_refs, q, t, nc):
    c = pl.program_id(1)
    tail = CONV_WIDTH - 1
    base = SUBLANES
    hp = D_SSD
    gw = D_SSD // SSD_GROUPS
    e_per_g = SSD_HEADS // SSD_GROUPS

    @pl.when(c == 0)
    def _():
        xin_ref[0:base, :] = jnp.zeros((base, D_XBC), F32)
        xin_ref[base - tail:base, :] = cs_ref[0]
        ht_ref[...] = h0_ref[0].reshape(hp, SSD_STATE).T

    if t < q:
        xin_ref[base:base + q, :] = jnp.zeros((q, D_XBC), F32)
        dtin_ref[...] = jnp.zeros((q, LANES), F32)
        (zin_ref,) = pad_refs
        zin_ref[...] = jnp.zeros((q, D_SSD), F32)
        zin_ref[0:t, :] = z_ref[0].astype(F32)
    xin_ref[base:base + t, 0:D_SSD] = x_ref[0].astype(F32)
    xin_ref[base:base + t, D_SSD:D_SSD + D_BC] = bm_ref[0].astype(F32)
    xin_ref[base:base + t, D_SSD + D_BC:D_XBC] = cm_ref[0].astype(F32)
    dtin_ref[0:t, :] = dt_ref[0]

    def conv(lo, hi):
        acc = cb_ref[:, lo:hi]
        for j in range(CONV_WIDTH):
            acc = acc + cw_ref[j:j + 1, lo:hi] * xin_ref[base - tail + j:base - tail + j + q, lo:hi]
        return _silu(acc)

    dt = _softplus(dtin_ref[...] + dtb_ref[...])
    if t < q:
        rows = lax.broadcasted_iota(jnp.int32, (q, LANES), 0)
        dt = jnp.where(rows < t, dt, 0.0)
    a = dt * a_ref[...]
    ri = lax.broadcasted_iota(jnp.int32, (q, q), 0)
    ci = lax.broadcasted_iota(jnp.int32, (q, q), 1)
    causal = ri >= ci
    tril = jnp.where(causal, 1.0, 0.0).astype(BF16)
    acum = _split_dot(a, tril, 3, left=True)
    acum_t = acum.T
    a_last = acum[q - 1:q, :]
    e_mat = e_ref[...]
    dt_e = _split_dot(dt, e_mat, 3)
    ea_e = _split_dot(jnp.exp(acum), e_mat, 3)
    ds_e = _split_dot(jnp.exp(a_last - acum), e_mat, 3)

    bmat = conv(D_SSD, D_SSD + D_BC)
    cmat = conv(D_SSD + D_BC, D_XBC).astype(BF16)
    bt = bmat.T.astype(BF16)

    yn_parts = []
    for g in range(SSD_GROUPS):
        gs = slice(g * gw, (g + 1) * gw)
        xs = conv(g * gw, (g + 1) * gw)
        xdt = xs * dt_e[:, gs]
        xdt_b = xdt.astype(BF16)
        cg = cmat[:, g * SSD_STATE:(g + 1) * SSD_STATE]
        bgt = bt[g * SSD_STATE:(g + 1) * SSD_STATE, :]
        cb = jnp.dot(cg, bgt, preferred_element_type=F32)
        ht_g = ht_ref[:, gs]
        y_off = jnp.dot(cg, ht_g.astype(BF16), preferred_element_type=F32)
        for e in range(e_per_g):
            h = g * e_per_g + e
            seg = acum[:, h:h + 1] - acum_t[h:h + 1, :]
            lmat = jnp.exp(jnp.where(causal, seg, -jnp.inf))
            mh = (cb * lmat).astype(BF16)
            hs = slice(e * SSD_HEAD_DIM, (e + 1) * SSD_HEAD_DIM)
            yd_ref[:, h * SSD_HEAD_DIM:(h + 1) * SSD_HEAD_DIM] = jnp.dot(
                mh, xdt_b[:, hs], preferred_element_type=F32)
        y = yd_ref[:, gs] + y_off * ea_e[:, gs] + dsk_ref[:, gs] * xs
        xw = (xdt * ds_e[:, gs]).astype(BF16)
        ht_ref[:, gs] = ht_g * ea_e[q - 1:q, gs] + jnp.dot(bgt, xw, preferred_element_type=F32)
        zg = z_ref[0][:, gs].astype(F32) if t == q else zin_ref[:, gs]
        yg = y * _silu(zg)
        ms = jnp.mean(yg * yg, axis=-1, keepdims=True)
        yn_parts.append((yg * lax.rsqrt(ms + EPS) * nw_ref[:, gs]).astype(BF16))
    yn = jnp.concatenate(yn_parts, axis=-1)
    br = jnp.dot(yn, wbr_ref[...], preferred_element_type=F32)
    y_ref[0] = br[0:t, :].astype(y_ref.dtype)

    @pl.when(c == nc - 1)
    def _():
        cso_ref[0] = xin_ref[base + t - tail:base + t, :]
        ho_ref[0] = ht_ref[...].T.reshape(SSD_HEADS, SSD_HEAD_DIM, SSD_STATE)

    if t == q:
        xin_ref[base - tail:base, :] = xin_ref[base + q - tail:base + q, :]


def _ssd(u3, dt3, conv_state, ssm_state, p, out_dtype):
    b, s, _ = u3.shape
    q = SSD_CHUNK
    t = min(q, s)
    nc = s // t
    assert s == nc * t and (t == q or nc == 1)

    def ucol(width, off):
        return pl.BlockSpec((1, t, width), lambda bi, ci: (bi, ci, off // width))

    scratch = [
        pltpu.VMEM((SUBLANES + q, D_XBC), F32),
        pltpu.VMEM((q, LANES), F32),
        pltpu.VMEM((SSD_STATE, D_SSD), F32),
        pltpu.VMEM((q, D_SSD), F32),
    ]
    if t < q:
        scratch.append(pltpu.VMEM((q, D_SSD), F32))
    return pl.pallas_call(
        functools.partial(_ssd_kernel, q=q, t=t, nc=nc),
        grid=(b, nc),
        in_specs=[
            ucol(D_SSD, U_Z), ucol(D_SSD, U_X), ucol(D_BC, U_B), ucol(D_BC, U_C),
            pl.BlockSpec((1, t, LANES), lambda bi, ci: (bi, ci, 0)),
            pl.BlockSpec((1, CONV_WIDTH - 1, D_XBC), lambda bi, ci: (bi, 0, 0)),
            pl.BlockSpec((1, SSD_HEADS, SSD_HEAD_DIM, SSD_STATE), lambda bi, ci: (bi, 0, 0, 0)),
            _const_spec((CONV_WIDTH, D_XBC)), _const_spec((1, D_XBC)),
            _const_spec((1, LANES)), _const_spec((1, LANES)),
            _const_spec((1, D_SSD)), _const_spec((1, D_SSD)),
            _const_spec((LANES, D_SSD)), _const_spec((D_SSD, D_MODEL)),
        ],
        out_specs=[
            pl.BlockSpec((1, t, D_MODEL), lambda bi, ci: (bi, ci, 0)),
            pl.BlockSpec((1, CONV_WIDTH - 1, D_XBC), lambda bi, ci: (bi, 0, 0)),
            pl.BlockSpec((1, SSD_HEADS, SSD_HEAD_DIM, SSD_STATE), lambda bi, ci: (bi, 0, 0, 0)),
        ],
        out_shape=[
            jax.ShapeDtypeStruct((b, s, D_MODEL), out_dtype),
            jax.ShapeDtypeStruct((b, CONV_WIDTH - 1, D_XBC), F32),
            jax.ShapeDtypeStruct((b, SSD_HEADS, SSD_HEAD_DIM, SSD_STATE), F32),
        ],
        scratch_shapes=scratch,
        compiler_params=pltpu.CompilerParams(
            dimension_semantics=("parallel", "arbitrary"), vmem_limit_bytes=VMEM_LIMIT),
        name="ssd",
    )(u3, u3, u3, u3, dt3, conv_state, ssm_state, p["conv_w"], p["conv_b"], p["dt_bias"], p["a_neg"],
      p["d_skip"], p["ssd_norm_w"], p["head_expand"], p["w_ssd_br"])


def _block_pieces(win, dil):
    if win <= IN_TM:
        return [(0, win, ATT_BLOCK, ATT_BLOCK)]
    per = IN_TM // dil
    return [(t * IN_TM, 0, per, per) for t in range(win // IN_TM)]


def _attn_prompt_kernel(q_ref, k_ref, v_ref, o_ref, l_ref, kbuf, vbuf, operm, onat, lperm, ob_ref, lb_ref,
                        *, win, dil):
    n = pl.program_id(1)
    sb, blk, hd = ATT_SB, ATT_BLOCK, ATT_HEAD_DIM
    nsub = sb // win
    pieces = _block_pieces(win, dil)
    cur = pl.multiple_of((n % 2) * sb, sb)
    prv = pl.multiple_of(sb - (n % 2) * sb, sb)
    kbuf[pl.ds(cur, sb), :] = k_ref[0]
    vbuf[pl.ds(cur, sb), :] = v_ref[0]

    @pl.when(n == 0)
    def _():
        kbuf[pl.ds(prv, sb), :] = jnp.zeros((sb, D_ATT), BF16)
        vbuf[pl.ds(prv, sb), :] = jnp.zeros((sb, D_ATT), BF16)

    lb_ref[...] = jnp.zeros((blk, LANES), F32)
    ri = lax.broadcasted_iota(jnp.int32, (blk, blk), 0)
    ci = lax.broadcasted_iota(jnp.int32, (blk, blk), 1)
    upper = ci > ri
    diag = ci == ri
    nt = (((1,), (1,)), ((), ()))

    def rows(ref, lead, off, sub, r):
        parts = []
        for base, per_sub, per_r, cnt in pieces:
            start = pl.multiple_of(off + base + sub * per_sub + r * per_r, cnt)
            idx = (pl.ds(start, cnt), slice(None))
            parts.append(ref[lead + idx] if lead else ref[idx])
        return parts[0] if len(parts) == 1 else jnp.concatenate(parts, axis=0)

    def block(t, carry):
        sub, r = t // dil, t % dil
        qb = rows(q_ref, (0,), 0, sub, r)
        kc = rows(kbuf, (), cur, sub, r)
        vc = rows(vbuf, (), cur, sub, r)
        poff = jnp.where(sub > 0, cur, prv)
        psub = jnp.where(sub > 0, sub - 1, nsub - 1)
        kp = rows(kbuf, (), poff, psub, r)
        vp = rows(vbuf, (), poff, psub, r)
        no_prev = jnp.logical_and(n == 0, sub == 0)
        bias = jnp.where(no_prev, -jnp.inf, 0.0).astype(F32)
        for h in range(ATT_SLOTS):
            hs = slice(h * hd, (h + 1) * hd)
            s_c = lax.dot_general(qb[:, hs], kc[:, hs], nt, preferred_element_type=F32)
            s_p = lax.dot_general(qb[:, hs], kp[:, hs], nt, preferred_element_type=F32) + bias
            sc = jnp.where(upper, s_p, s_c)
            far = jnp.sum(jnp.where(diag, s_p, 0.0), axis=-1, keepdims=True)
            m = jnp.maximum(jnp.max(sc, axis=-1, keepdims=True), far)
            p = jnp.exp(sc - m)
            p_far = jnp.exp(far - m)
            den = jnp.sum(p, axis=-1, keepdims=True) + p_far
            pb = p.astype(BF16)
            zero = jnp.zeros_like(pb)
            o_h = (jnp.dot(jnp.where(upper, pb, zero), vp[:, hs], preferred_element_type=F32)
                   + jnp.dot(jnp.where(upper, zero, pb), vc[:, hs], preferred_element_type=F32)
                   + p_far * vp[:, hs].astype(F32)) / den
            ob_ref[:, hs] = o_h
            lb_ref[:, h:h + 1] = m + jnp.log(den)
        start = pl.multiple_of(t * blk, blk)
        operm[pl.ds(start, blk), :] = ob_ref[...]
        lperm[pl.ds(start, blk), :] = lb_ref[...]
        return carry

    lax.fori_loop(0, sb // blk, block, 0)

    nslab = D_ATT // LANES
    for sub in range(nsub):
        for r in range(dil):
            src = (sub * dil + r) * blk
            dst = pl.ds(sub * win + r, blk, stride=dil) if dil > 1 else pl.ds(src, blk)
            for c in range(nslab):
                onat[c, dst, :] = operm[src:src + blk, c * LANES:(c + 1) * LANES]
            l_ref[0, dst, :] = lperm[src:src + blk, :]
    for c in range(nslab):
        o_ref[0, :, c * LANES:(c + 1) * LANES] = onat[c].astype(o_ref.dtype)


def _attn_prompt(u2d, b, s, gi):
    win, dil = ATT_GROUPS[gi]
    sb = ATT_SB
    nsb = s // sb
    assert win // dil == ATT_BLOCK and s == nsb * sb and sb % win == 0
    ncol = N_U // D_ATT
    col0 = (U_ATT + gi * U_TILE) // D_ATT

    def ucol(k):
        return pl.BlockSpec((1, sb, D_ATT), lambda bi, n: (bi, n, col0 + k))

    tok = lambda width: pl.BlockSpec((1, sb, width), lambda bi, n: (bi, n, 0))
    u3 = u2d.reshape(b, s, N_U)
    o, lse = pl.pallas_call(
        functools.partial(_attn_prompt_kernel, win=win, dil=dil),
        grid=(b, nsb),
        in_specs=[ucol(0), ucol(1), ucol(2)],
        out_specs=[tok(D_ATT), tok(LANES)],
        out_shape=[
            jax.ShapeDtypeStruct((b, s, D_ATT), BF16),
            jax.ShapeDtypeStruct((b, s, LANES), F32),
        ],
        scratch_shapes=[
            pltpu.VMEM((2 * sb, D_ATT), BF16), pltpu.VMEM((2 * sb, D_ATT), BF16),
            pltpu.VMEM((sb, D_ATT), F32), pltpu.VMEM((D_ATT // LANES, sb, LANES), F32),
            pltpu.VMEM((sb, LANES), F32),
            pltpu.VMEM((ATT_BLOCK, D_ATT), F32), pltpu.VMEM((ATT_BLOCK, LANES), F32),
        ],
        compiler_params=pltpu.CompilerParams(
            dimension_semantics=("parallel", "arbitrary"), vmem_limit_bytes=VMEM_LIMIT),
        name=f"attn_prompt_g{gi}",
    )(u3, u3, u3)
    return o.reshape(b * s, D_ATT), lse.reshape(b * s, LANES)


def _kv_tail_kernel(x_ref, o_ref, nat_ref, *, win, dil):
    span = min(win, IN_TM)
    per_slab = LANES // ATT_HEAD_DIM
    for c in range(D_ATT // LANES):
        cs = slice(c * LANES, (c + 1) * LANES)
        if dil == 1:
            nat = x_ref[0, :, cs].astype(F32)
        else:
            for t in range(win // span):
                for dst, src, cnt in _residue_pieces(win, dil, span):
                    nat_ref[c, pl.ds(t * span + src, cnt, stride=dil), :] = (
                        x_ref[0, t * span + dst:t * span + dst + cnt, cs].astype(F32))
            nat = nat_ref[c]
        o_ref[0, 0, c * per_slab:(c + 1) * per_slab] = nat.T.reshape(per_slab, ATT_HEAD_DIM, win)


def _kv_tail(u2d, b, s, gi):
    win, dil = ATT_GROUPS[gi]
    assert s % win == 0 and (win <= IN_TM or win % IN_TM == 0)
    col0 = (U_ATT + gi * U_TILE) // D_ATT + 1
    last = s // win - 1
    u3 = u2d.reshape(b, s, N_U)
    return pl.pallas_call(
        functools.partial(_kv_tail_kernel, win=win, dil=dil),
        grid=(b, 2),
        in_specs=[pl.BlockSpec((1, win, D_ATT), lambda bi, kv: (bi, last, col0 + kv))],
        out_specs=pl.BlockSpec((1, 1, ATT_SLOTS, ATT_HEAD_DIM, win), lambda bi, kv: (bi, kv, 0, 0, 0)),
        out_shape=jax.ShapeDtypeStruct((b, 2, ATT_SLOTS, ATT_HEAD_DIM, win), F32),
        scratch_shapes=[pltpu.VMEM((D_ATT // LANES, win, LANES), F32)],
        compiler_params=pltpu.CompilerParams(
            dimension_semantics=("parallel", "parallel"), vmem_limit_bytes=VMEM_LIMIT),
        name=f"kv_tail_g{gi}",
    )(u3)


def _attn_decode_kernel(x_ref, c0_ref, c1_ref, c2_ref, o_ref, n0_ref, n1_ref, n2_ref):
    caches = (c0_ref, c1_ref, c2_ref)
    news = (n0_ref, n1_ref, n2_ref)
    ng = N_ATT_GROUPS

    outs_g, lses = [], []
    for g, (win, dil) in enumerate(ATT_GROUPS):
        qc = x_ref[0, 0, g]
        kc = x_ref[0, 0, ng + g]
        vc = x_ref[0, 0, 2 * ng + g]
        kt = caches[g][0, 0, 0]
        vt = caches[g][0, 1, 0]
        lane = lax.broadcasted_iota(jnp.int32, (1, win), 1)
        sc = jnp.sum(kt * qc, axis=0, keepdims=True)
        sc = jnp.where(lane % dil == 0, sc, -jnp.inf)
        sc_new = jnp.sum(kc * qc, axis=0, keepdims=True)
        m = jnp.maximum(jnp.max(sc, axis=1, keepdims=True), sc_new)
        p = jnp.exp(sc - m)
        p_new = jnp.exp(sc_new - m)
        den = jnp.sum(p, axis=1, keepdims=True) + p_new
        outs_g.append((jnp.sum(vt * p, axis=1, keepdims=True) + p_new * vc) / den)
        lses.append(m + jnp.log(den))
        last = lane == win - 1
        news[g][0, 0, 0] = jnp.where(last, kc, pltpu.roll(kt, win - 1, axis=1))
        news[g][0, 1, 0] = jnp.where(last, vc, pltpu.roll(vt, win - 1, axis=1))

    top = jnp.maximum(jnp.maximum(lses[0], lses[1]), lses[2])
    ws = [jnp.exp(l - top) for l in lses]
    o_ref[0, 0] = (ws[0] * outs_g[0] + ws[1] * outs_g[1] + ws[2] * outs_g[2]) / (ws[0] + ws[1] + ws[2])


def _attn_decode(us, caches):
    b = us.shape[0]
    ng = N_ATT_GROUPS
    hd = ATT_HEAD_DIM
    x = us[:, U_ATT:].reshape(b, ng, 3, ATT_SLOTS, hd)
    x = x.transpose(0, 3, 2, 1, 4).reshape(b, ATT_SLOTS, 3 * ng, hd, 1)
    cts = [c.transpose(0, 2, 3, 4, 1) for c in caches]
    cache_specs = []
    for c, (win, dil) in zip(cts, ATT_GROUPS):
        assert c.shape == (b, 2, ATT_SLOTS, hd, win) and win // dil == ATT_BLOCK
        cache_specs.append(pl.BlockSpec((1, 2, 1, hd, win), lambda bi, h: (bi, 0, h, 0, 0)))
    res = pl.pallas_call(
        _attn_decode_kernel,
        grid=(b, ATT_SLOTS),
        in_specs=[pl.BlockSpec((1, 1, 3 * ng, hd, 1), lambda bi, h: (bi, h, 0, 0, 0))] + cache_specs,
        out_specs=[pl.BlockSpec((1, 1, hd, 1), lambda bi, h: (bi, h, 0, 0))] + cache_specs,
        out_shape=[jax.ShapeDtypeStruct((b, ATT_SLOTS, hd, 1), F32)]
        + [jax.ShapeDtypeStruct(c.shape, F32) for c in cts],
        compiler_params=pltpu.CompilerParams(
            dimension_semantics=("parallel", "parallel"), vmem_limit_bytes=VMEM_LIMIT),
        name="attn_decode",
    )(x, *cts)
    o = res[0].reshape(b, D_ATT)
    kvs = [r.transpose(0, 4, 1, 2, 3) for r in res[1:]]
    return o, kvs


def _out_kernel(*refs, n_groups):
    o_refs = refs[:n_groups]
    l_refs = refs[n_groups:2 * n_groups - (n_groups == 1)]
    z_ref, br_ref, gs_ref, ga_ref, x_ref, he_ref, wa_ref, wo_ref, y_ref = refs[len(o_refs) + len(l_refs):]
    if n_groups == 1:
        o = o_refs[0][...].astype(F32)
    else:
        ls = [l[...] for l in l_refs]
        top = functools.reduce(jnp.maximum, ls)
        ws = [jnp.exp(l - top) for l in ls]
        den = functools.reduce(lambda a, c: a + c, ws)
        o = None
        for w, o_ref in zip(ws, o_refs):
            term = _split_dot(w / den, he_ref[...], 2) * o_ref[...].astype(F32)
            o = term if o is None else o + term
    o = o * _silu(z_ref[...].astype(F32))
    br_att = jnp.dot(o.astype(BF16), wa_ref[...], preferred_element_type=F32)
    mix = (_sigmoid(gs_ref[...].astype(F32)) * br_ref[...].astype(F32)
           + _sigmoid(ga_ref[...].astype(F32)) * br_att)
    y_ref[...] = x_ref[...] + jnp.dot(mix.astype(BF16), wo_ref[...], preferred_element_type=F32)


def _out_proj(os_, lses, u2d, br_ssd, x2d, p, tm):
    m = x2d.shape[0]
    row = lambda width, off=0: pl.BlockSpec((tm, width), lambda i: (i, off // width))
    return pl.pallas_call(
        functools.partial(_out_kernel, n_groups=len(os_)),
        grid=(m // tm,),
        in_specs=[row(D_ATT)] * len(os_) + [row(LANES)] * len(lses)
        + [row(D_ATT, U_ZATT), row(D_MODEL), row(D_MODEL, U_GSSD), row(D_MODEL, U_GATT), row(D_MODEL),
           _const_spec((LANES, D_ATT)), _const_spec((D_ATT, D_MODEL)), _const_spec((D_MODEL, D_MODEL))],
        out_specs=row(D_MODEL),
        out_shape=jax.ShapeDtypeStruct((m, D_MODEL), F32),
        compiler_params=pltpu.CompilerParams(
            dimension_semantics=("parallel",), vmem_limit_bytes=VMEM_LIMIT),
        name="out_proj",
    )(*os_, *lses, u2d, br_ssd, u2d, u2d, x2d, p["slot_expand"], p["w_att_br"], p["w_out"])


def _rope_tables(pos):
    half = ATT_HEAD_DIM // 2
    inv = ROPE_THETA ** (-jnp.arange(half, dtype=F32) / half)
    ang = pos.astype(F32)[:, None] * inv[None, :]
    cos, sin = jnp.cos(ang), jnp.sin(ang)
    cos = jnp.concatenate([cos, cos], axis=-1)
    sin = jnp.concatenate([-sin, sin], axis=-1)
    return jnp.tile(cos, (1, LANES // ATT_HEAD_DIM)), jnp.tile(sin, (1, LANES // ATT_HEAD_DIM))


def _storage_positions(s, tm):
    out = []
    for win, dil in ATT_GROUPS:
        idx = np.arange(s)
        if dil > 1:
            for t0 in range(0, s, tm):
                for dst, src, cnt in _residue_pieces(win, dil, tm):
                    idx[t0 + dst:t0 + dst + cnt] = t0 + src + dil * np.arange(cnt)
        out.append(idx)
    return np.stack(out)


def _layer_params(l, norm_w, w_in, conv_w, conv_b, dt_bias, a_log, d_skip, ssd_norm_w, q_norm_w,
                  k_norm_w, w_ssd_br, w_att_br, w_out):
    w = w_in[l]
    o_dt = D_SSD + D_XBC
    o_q = o_dt + SSD_HEADS
    o_k, o_v = o_q + D_ATT_QK, o_q + 2 * D_ATT_QK
    o_zatt = o_q + 3 * D_ATT_QK
    o_gssd, o_gatt = o_zatt + D_ATT, o_zatt + D_ATT + D_MODEL
    cols = [w[:, :o_dt], w[:, o_gssd:o_gatt], w[:, o_gatt:o_gatt + D_MODEL], w[:, o_zatt:o_gssd]]
    for g in range(N_ATT_GROUPS):
        cols += [w[:, o + g * D_ATT:o + (g + 1) * D_ATT] for o in (o_q, o_k, o_v)]
    pad = LANES - SSD_HEADS
    lane_i = np.arange(LANES)
    return dict(
        norm_w=norm_w[l][None, :],
        w_main=jnp.concatenate(cols, axis=1).astype(BF16),
        w_dt=jnp.pad(w[:, o_dt:o_q], ((0, 0), (0, pad))).astype(BF16),
        conv_w=conv_w[l], conv_b=conv_b[l][None, :],
        dt_bias=jnp.pad(dt_bias[l].astype(F32), (0, pad))[None, :],
        a_neg=jnp.pad(-jnp.exp(a_log[l].astype(F32)), (0, pad))[None, :],
        d_skip=jnp.repeat(d_skip[l].astype(F32), SSD_HEAD_DIM)[None, :],
        ssd_norm_w=ssd_norm_w[l][None, :],
        head_expand=jnp.asarray(
            lane_i[:, None] == (np.arange(D_SSD) // SSD_HEAD_DIM)[None, :], BF16),
        slot_expand=jnp.asarray(
            lane_i[:, None] == (np.arange(D_ATT) // ATT_HEAD_DIM)[None, :], BF16),
        head_seg=jnp.asarray(
            lane_i[:, None] // ATT_HEAD_DIM == lane_i[None, :] // ATT_HEAD_DIM, BF16),
        qk_norm_w=jnp.concatenate([jnp.tile(q_norm_w[l].astype(F32) * ATT_SCALE, ATT_SLOTS),
                                   jnp.tile(k_norm_w[l].astype(F32), ATT_SLOTS)])[None, :],
        w_ssd_br=w_ssd_br[l].astype(BF16), w_att_br=w_att_br[l].astype(BF16),
        w_out=w_out[l].astype(BF16),
    )


def _prompt_layer(x, p):
    b, s, _ = x.shape
    assert s % ATT_SB == 0 and ATT_SB % IN_TM == 0
    x2d = x.reshape(b * s, D_MODEL)
    cos, sin = _rope_tables(jnp.arange(s, dtype=jnp.int32))
    order = _storage_positions(s, IN_TM)
    u, dt = _inproj(x2d, p, cos[order], sin[order], BF16, IN_TM, True)
    conv0 = jnp.zeros((b, CONV_WIDTH - 1, D_XBC), F32)
    ssm0 = jnp.zeros((b, SSD_HEADS, SSD_HEAD_DIM, SSD_STATE), F32)
    br_ssd, conv_new, ssm_new = _ssd(u.reshape(b, s, N_U), dt.reshape(b, s, LANES), conv0, ssm0, p, BF16)
    os_, lses, kvs = [], [], []
    for gi in range(N_ATT_GROUPS):
        o, lse = _attn_prompt(u, b, s, gi)
        os_.append(o)
        lses.append(lse)
        kvs.append(_kv_tail(u, b, s, gi).transpose(0, 4, 1, 2, 3))
    y = _out_proj(os_, lses, u, br_ssd.reshape(b * s, D_MODEL), x2d, p, 512)
    return y.reshape(b, s, D_MODEL), conv_new, ssm_new, kvs


def _decode_layer(x, conv_state, ssm_state, caches, p):
    b, s, _ = x.shape
    assert s == 1
    x2d = x.reshape(b, D_MODEL)
    cos, sin = _rope_tables(jnp.full((b,), PAST_LEN, jnp.int32))
    cos3 = jnp.broadcast_to(cos[None], (N_ATT_GROUPS, b, LANES))
    sin3 = jnp.broadcast_to(sin[None], (N_ATT_GROUPS, b, LANES))
    u, dt = _inproj(x2d, p, cos3, sin3, F32, b, False)
    br_ssd, conv_new, ssm_new = _ssd(u.reshape(b, s, N_U), dt.reshape(b, s, LANES),
                                     conv_state, ssm_state, p, F32)
    o, kvs = _attn_decode(u, caches)
    y = _out_proj([o], [], u, br_ssd.reshape(b, D_MODEL), x2d, p, b)
    return y.reshape(b, s, D_MODEL), conv_new, ssm_new, kvs


def kernel(x_prompt, x_sample, state_conv, state_ssm, cache_kv_w128, cache_kv_w512, cache_kv_w2048,
           norm_w, w_in, conv_w, conv_b, dt_bias, a_log, d_skip, ssd_norm_w, q_norm_w, k_norm_w,
           w_ssd_br, w_att_br, w_out):
    yp, ys = x_prompt, x_sample
    outs_p = [[] for _ in range(5)]
    outs_s = [[] for _ in range(5)]
    for l in range(norm_w.shape[0]):
        p = _layer_params(l, norm_w, w_in, conv_w, conv_b, dt_bias, a_log, d_skip, ssd_norm_w,
                          q_norm_w, k_norm_w, w_ssd_br, w_att_br, w_out)
        yp, c, h, kv = _prompt_layer(yp, p)
        for acc, val in zip(outs_p, [c, h] + kv):
            acc.append(val)
        ys, c, h, kv = _decode_layer(ys, state_conv[l], state_ssm[l],
                                     [cache_kv_w128[l], cache_kv_w512[l], cache_kv_w2048[l]], p)
        for acc, val in zip(outs_s, [c, h] + kv):
            acc.append(val)
    return (yp, ys, *[jnp.stack(a) for a in outs_p], *[jnp.stack(a) for a in outs_s])
```

```python
import functools

import numpy as np
import jax
import jax.numpy as jnp
from jax import lax
from jax.experimental import pallas as pl
from jax.experimental.pallas import tpu as pltpu

F32 = jnp.float32
BF16 = jnp.bfloat16

D_MODEL = 1024
D_SSD = 2048
SSD_HEADS = 32
SSD_HEAD_DIM = 64
SSD_GROUPS = 4
SSD_STATE = 128
CONV_WIDTH = 4
D_BC = SSD_GROUPS * SSD_STATE
D_XBC = D_SSD + 2 * D_BC
ATT_HEAD_DIM = 64
ATT_SLOTS = 8
ATT_GROUPS = ((128, 1), (512, 4), (2048, 16))
N_ATT_GROUPS = len(ATT_GROUPS)
D_ATT = ATT_SLOTS * ATT_HEAD_DIM
D_ATT_QK = N_ATT_GROUPS * D_ATT
ATT_SCALE = ATT_HEAD_DIM ** -0.5
ROPE_THETA = 10000.0
EPS = 1e-6
PAST_LEN = 16384

U_Z, U_X, U_B, U_C, U_GSSD, U_GATT, U_ZATT = 0, 2048, 4096, 4608, 5120, 6144, 7168
U_ATT = 7680
U_TILE = 3 * D_ATT
N_U = U_ATT + N_ATT_GROUPS * U_TILE
N_PLAIN_TILES = U_ATT // U_TILE

LANES = 128
SUBLANES = 8
SSD_CHUNK = 128
ATT_BLOCK = 128
ATT_SB = 2048
IN_TM = 1024
VMEM_LIMIT = 56 * 1024 * 1024


def _split_dot(a, m, terms, left=False):
    out = None
    r = a
    for t in range(terms):
        p = r.astype(BF16)
        d = (jnp.dot(m, p, preferred_element_type=F32) if left
             else jnp.dot(p, m, preferred_element_type=F32))
        out = d if out is None else out + d
        if t + 1 < terms:
            r = r - p.astype(F32)
    return out


def _sigmoid(x):
    return 1.0 / (1.0 + jnp.exp(-x))


def _silu(x):
    return x * _sigmoid(x)


def _softplus(x):
    return jnp.maximum(x, 0.0) + jnp.log1p(jnp.exp(-jnp.abs(x)))


def _const_spec(shape):
    return pl.BlockSpec(shape, lambda *_: (0,) * len(shape))


def _residue_pieces(win, dil, tm):
    span = min(win, tm)
    per = span // dil
    return [(w * span + r * per, w * span + r, per)
            for w in range(max(tm // win, 1)) for r in range(dil)]


def _inproj_kernel(x_ref, nw_ref, w_ref, wdt_ref, cos_ref, sin_ref, qkw_ref, seg_ref,
                   u_ref, dt_ref, hn_ref, *perm_refs, tm, permute):
    j = pl.program_id(1)

    @pl.when(j == 0)
    def _():
        x = x_ref[...]
        ms = jnp.mean(x * x, axis=-1, keepdims=True)
        hn = x * lax.rsqrt(ms + EPS) * nw_ref[...]
        hn_ref[0] = hn.astype(BF16)
        dt_ref[...] = jnp.dot(hn_ref[0], wdt_ref[...], preferred_element_type=F32)
        if permute:
            (hnf_ref,) = perm_refs
            for c in range(D_MODEL // LANES):
                hnf_ref[c] = hn[:, c * LANES:(c + 1) * LANES]
            for g, (win, dil) in enumerate(ATT_GROUPS):
                if dil == 1:
                    continue
                for dst, src, cnt in _residue_pieces(win, dil, tm):
                    for c in range(D_MODEL // LANES):
                        hn_ref[g, dst:dst + cnt, c * LANES:(c + 1) * LANES] = (
                            hnf_ref[c, pl.ds(src, cnt, stride=dil), :].astype(BF16))

    @pl.when(j < N_PLAIN_TILES)
    def _():
        u_ref[...] = jnp.dot(hn_ref[0], w_ref[...], preferred_element_type=F32).astype(u_ref.dtype)

    half = ATT_HEAD_DIM // 2
    for g, (_, dil) in enumerate(ATT_GROUPS):
        @pl.when(j == N_PLAIN_TILES + g)
        def _(g=g, dil=dil):
            src = g if (permute and dil > 1) else 0
            acc = jnp.dot(hn_ref[src], w_ref[...], preferred_element_type=F32)
            cos, sin, seg = cos_ref[0], sin_ref[0], seg_ref[...]
            lane = lax.broadcasted_iota(jnp.int32, (tm, LANES), 1)
            first = lane % ATT_HEAD_DIM < half
            for c in range(2 * D_ATT // LANES):
                cs = slice(c * LANES, (c + 1) * LANES)
                xs = acc[:, cs]
                ss = _split_dot(xs * xs, seg, 2)
                xn = xs * lax.rsqrt(ss * (1.0 / ATT_HEAD_DIM) + EPS) * qkw_ref[:, cs]
                partner = jnp.where(first, pltpu.roll(xn, LANES - half, axis=1),
                                    pltpu.roll(xn, half, axis=1))
                u_ref[:, cs] = (xn * cos + partner * sin).astype(u_ref.dtype)
            u_ref[:, 2 * D_ATT:] = acc[:, 2 * D_ATT:].astype(u_ref.dtype)


def _inproj(x2d, p, cos3, sin3, out_dtype, tm, permute):
    m = x2d.shape[0]
    period = cos3.shape[1] // tm
    tab = pl.BlockSpec((1, tm, LANES),
                       lambda i, j: (jnp.clip(j - N_PLAIN_TILES, 0, N_ATT_GROUPS - 1), i % period, 0))
    scratch = [pltpu.VMEM((N_ATT_GROUPS if permute else 1, tm, D_MODEL), BF16)]
    if permute:
        scratch.append(pltpu.VMEM((D_MODEL // LANES, tm, LANES), F32))
    return pl.pallas_call(
        functools.partial(_inproj_kernel, tm=tm, permute=permute),
        grid=(m // tm, N_U // U_TILE),
        in_specs=[
            pl.BlockSpec((tm, D_MODEL), lambda i, j: (i, 0)),
            _const_spec((1, D_MODEL)),
            pl.BlockSpec((D_MODEL, U_TILE), lambda i, j: (0, j)),
            _const_spec((D_MODEL, LANES)),
            tab, tab,
            _const_spec((1, 2 * D_ATT)),
            _const_spec((LANES, LANES)),
        ],
        out_specs=[
            pl.BlockSpec((tm, U_TILE), lambda i, j: (i, j)),
            pl.BlockSpec((tm, LANES), lambda i, j: (i, 0)),
        ],
        out_shape=[
            jax.ShapeDtypeStruct((m, N_U), out_dtype),
            jax.ShapeDtypeStruct((m, LANES), F32),
        ],
        scratch_shapes=scratch,
        compiler_params=pltpu.CompilerParams(
            dimension_semantics=("parallel", "arbitrary"), vmem_limit_bytes=VMEM_LIMIT),
        name="inproj",
    )(x2d, p["norm_w"], p["w_main"], p["w_dt"], cos3, sin3, p["qk_norm_w"], p["head_seg"])


def _ssd_kernel(z_ref, x_ref, bm_ref, cm_ref, dt_ref, cs_ref, h0_ref, cw_ref, cb_ref, dtb_ref,
                a_ref, dsk_ref, nw_ref, e_ref, sh_ref, wbr_ref, y_ref, cso_ref, ho_ref,
                xin_ref, dtin_ref, ht_ref, yd_ref, *pad_refs, q, t, nc):
    c = pl.program_id(1)
    tail = CONV_WIDTH - 1
    base = SUBLANES
    hp = D_SSD
    gw = D_SSD // SSD_GROUPS
    e_per_g = SSD_HEADS // SSD_GROUPS
    full = t == q

    @pl.when(c == 0)
    def _():
        xin_ref[0:2 * base if full else base, :] = jnp.zeros((2 * base if full else base, D_XBC), F32)
        xin_ref[base - tail:base, :] = cs_ref[0]
        ht_ref[...] = h0_ref[0].reshape(hp, SSD_STATE).T

    if not full:
        xin_ref[base:base + q, :] = jnp.zeros((q, D_XBC), F32)
        dtin_ref[...] = jnp.zeros((q, LANES), F32)
        (zin_ref,) = pad_refs
        zin_ref[...] = jnp.zeros((q, D_SSD), F32)
        zin_ref[0:t, :] = z_ref[0].astype(F32)
        xin_ref[base:base + t, 0:D_SSD] = x_ref[0].astype(F32)
        xin_ref[base:base + t, D_SSD:D_SSD + D_BC] = bm_ref[0].astype(F32)
        xin_ref[base:base + t, D_SSD + D_BC:D_XBC] = cm_ref[0].astype(F32)
        dtin_ref[0:t, :] = dt_ref[0]

    def conv(src_ref, off, lo, hi):
        if not full:
            acc = cb_ref[:, lo:hi]
            for j in range(CONV_WIDTH):
                acc = acc + cw_ref[j:j + 1, lo:hi] * xin_ref[base - tail + j:base - tail + j + q, lo:hi]
            return _silu(acc)
        xb = src_ref[0][:, off:off + hi - lo]
        xf = xb.astype(F32)
        shifted = jnp.dot(sh_ref[...], xb, preferred_element_type=F32)
        acc = cb_ref[:, lo:hi] + cw_ref[tail:tail + 1, lo:hi] * xf
        head = acc[0:base]
        for j in range(tail):
            acc = acc + cw_ref[j:j + 1, lo:hi] * shifted[j * q:(j + 1) * q]
            head = head + cw_ref[j:j + 1, lo:hi] * (
                shifted[j * q:j * q + base] + xin_ref[base - tail + j:2 * base - tail + j, lo:hi])
        xin_ref[0:base, lo:hi] = xf[q - base:q]
        return _silu(jnp.concatenate([head, acc[base:]], axis=0))

    dt = _softplus((dt_ref[0] if full else dtin_ref[...]) + dtb_ref[...])
    if not full:
        rows = lax.broadcasted_iota(jnp.int32, (q, LANES), 0)
        dt = jnp.where(rows < t, dt, 0.0)
    a = dt * a_ref[...]
    ri = lax.broadcasted_iota(jnp.int32, (q, q), 0)
    ci = lax.broadcasted_iota(jnp.int32, (q, q), 1)
    causal = ri >= ci
    tril = jnp.where(causal, 1.0, 0.0).astype(BF16)
    acum = _split_dot(a, tril, 3, left=True)
    acum_t = acum.T
    a_last = acum[q - 1:q, :]
    e_mat = e_ref[...]
    dt_e = _split_dot(dt, e_mat, 2)
    ea_e = _split_dot(jnp.exp(acum), e_mat, 2)
    ds_e = _split_dot(jnp.exp(a_last - acum), e_mat, 2)

    bmat = conv(bm_ref, 0, D_SSD, D_SSD + D_BC)
    cmat = conv(cm_ref, 0, D_SSD + D_BC, D_XBC).astype(BF16)
    bt = bmat.T.astype(BF16)

    yn_parts = []
    for g in range(SSD_GROUPS):
        gs = slice(g * gw, (g + 1) * gw)
        xs = conv(x_ref, g * gw, g * gw, (g + 1) * gw)
        xdt = xs * dt_e[:, gs]
        xdt_b = xdt.astype(BF16)
        cg = cmat[:, g * SSD_STATE:(g + 1) * SSD_STATE]
        bgt = bt[g * SSD_STATE:(g + 1) * SSD_STATE, :]
        cb = jnp.dot(cg, bgt, preferred_element_type=F32)
        ht_g = ht_ref[:, gs]
        y_off = jnp.dot(cg, ht_g.astype(BF16), preferred_element_type=F32)
        for e in range(e_per_g):
            h = g * e_per_g + e
            seg = acum[:, h:h + 1] - acum_t[h:h + 1, :]
            lmat = jnp.exp(jnp.where(causal, seg, -jnp.inf))
            mh = (cb * lmat).astype(BF16)
            hs = slice(e * SSD_HEAD_DIM, (e + 1) * SSD_HEAD_DIM)
            yd_ref[:, h * SSD_HEAD_DIM:(h + 1) * SSD_HEAD_DIM] = jnp.dot(
                mh, xdt_b[:, hs], preferred_element_type=F32)
        y = yd_ref[:, gs] + y_off * ea_e[:, gs] + dsk_ref[:, gs] * xs
        xw = (xdt * ds_e[:, gs]).astype(BF16)
        ht_ref[:, gs] = ht_g * ea_e[q - 1:q, gs] + jnp.dot(bgt, xw, preferred_element_type=F32)
        zg = z_ref[0][:, gs].astype(F32) if t == q else zin_ref[:, gs]
        yg = y * _silu(zg)
        ms = jnp.mean(yg * yg, axis=-1, keepdims=True)
        yn_parts.append((yg * lax.rsqrt(ms + EPS) * nw_ref[:, gs]).astype(BF16))
    yn = jnp.concatenate(yn_parts, axis=-1)
    br = jnp.dot(yn, wbr_ref[...], preferred_element_type=F32)
    y_ref[0] = br[0:t, :].astype(y_ref.dtype)

    @pl.when(c == nc - 1)
    def _():
        last = base if full else base + t
        cso_ref[0] = xin_ref[last - tail:last, :]
        ho_ref[0] = ht_ref[...].T.reshape(SSD_HEADS, SSD_HEAD_DIM, SSD_STATE)


def _ssd(u3, dt3, conv_state, ssm_state, p, out_dtype):
    b, s, _ = u3.shape
    q = SSD_CHUNK
    t = min(q, s)
    nc = s // t
    assert s == nc * t and (t == q or nc == 1)

    def ucol(width, off):
        return pl.BlockSpec((1, t, width), lambda bi, ci: (bi, ci, off // width))

    tail = CONV_WIDTH - 1
    ti = np.arange(q)
    shift = np.concatenate([ti[None, :] == (ti[:, None] - (tail - j)) for j in range(tail)], axis=0)
    scratch = [
        pltpu.VMEM((2 * SUBLANES if t == q else SUBLANES + q, D_XBC), F32),
        pltpu.VMEM((q, LANES), F32),
        pltpu.VMEM((SSD_STATE, D_SSD), F32),
        pltpu.VMEM((q, D_SSD), F32),
    ]
    if t < q:
        scratch.append(pltpu.VMEM((q, D_SSD), F32))
    return pl.pallas_call(
        functools.partial(_ssd_kernel, q=q, t=t, nc=nc),
        grid=(b, nc),
        in_specs=[
            ucol(D_SSD, U_Z), ucol(D_SSD, U_X), ucol(D_BC, U_B), ucol(D_BC, U_C),
            pl.BlockSpec((1, t, LANES), lambda bi, ci: (bi, ci, 0)),
            pl.BlockSpec((1, CONV_WIDTH - 1, D_XBC), lambda bi, ci: (bi, 0, 0)),
            pl.BlockSpec((1, SSD_HEADS, SSD_HEAD_DIM, SSD_STATE), lambda bi, ci: (bi, 0, 0, 0)),
            _const_spec((CONV_WIDTH, D_XBC)), _const_spec((1, D_XBC)),
            _const_spec((1, LANES)), _const_spec((1, LANES)),
            _const_spec((1, D_SSD)), _const_spec((1, D_SSD)),
            _const_spec((LANES, D_SSD)), _const_spec((tail * q, q)), _const_spec((D_SSD, D_MODEL)),
        ],
        out_specs=[
            pl.BlockSpec((1, t, D_MODEL), lambda bi, ci: (bi, ci, 0)),
            pl.BlockSpec((1, CONV_WIDTH - 1, D_XBC), lambda bi, ci: (bi, 0, 0)),
            pl.BlockSpec((1, SSD_HEADS, SSD_HEAD_DIM, SSD_STATE), lambda bi, ci: (bi, 0, 0, 0)),
        ],
        out_shape=[
            jax.ShapeDtypeStruct((b, s, D_MODEL), out_dtype),
            jax.ShapeDtypeStruct((b, CONV_WIDTH - 1, D_XBC), F32),
            jax.ShapeDtypeStruct((b, SSD_HEADS, SSD_HEAD_DIM, SSD_STATE), F32),
        ],
        scratch_shapes=scratch,
        compiler_params=pltpu.CompilerParams(
            dimension_semantics=("parallel", "arbitrary"), vmem_limit_bytes=VMEM_LIMIT),
        name="ssd",
    )(u3, u3, u3, u3, dt3, conv_state, ssm_state, p["conv_w"], p["conv_b"], p["dt_bias"], p["a_neg"],
      p["d_skip"], p["ssd_norm_w"], p["head_expand"], jnp.asarray(shift, BF16), p["w_ssd_br"])


def _block_pieces(win, dil):
    if win <= IN_TM:
        return [(0, win, ATT_BLOCK, ATT_BLOCK)]
    per = IN_TM // dil
    return [(t * IN_TM, 0, per, per) for t in range(win // IN_TM)]


def _attn_prompt_kernel(q_ref, k_ref, v_ref, o_ref, l_ref, kbuf, vtbuf, operm, onat, lperm, lt_ref,
                        s_scr, p_scr, far_scr, inv_scr, *, win, dil):
    n = pl.program_id(1)
    sb, blk, hd = ATT_SB, ATT_BLOCK, ATT_HEAD_DIM
    nsub = sb // win
    nblk = sb // blk
    pieces = _block_pieces(win, dil)
    cur = pl.multiple_of((n % 2) * sb, sb)
    prv = pl.multiple_of(sb - (n % 2) * sb, sb)
    kbuf[pl.ds(cur, sb), :] = k_ref[0]

    @pl.when(n == 0)
    def _():
        kbuf[pl.ds(prv, sb), :] = jnp.zeros((sb, D_ATT), BF16)
        vtbuf[pl.ds(nblk, nblk)] = jnp.zeros((nblk, D_ATT, blk), BF16)

    lt_ref[...] = jnp.zeros((LANES, blk), F32)
    ki = lax.broadcasted_iota(jnp.int32, (blk, blk), 0)
    qi = lax.broadcasted_iota(jnp.int32, (blk, blk), 1)
    upper = ki > qi
    diag = ki == qi
    lane_head = lax.broadcasted_iota(jnp.int32, (blk, LANES), 1) // hd
    nt = (((1,), (1,)), ((), ()))

    def rows(ref, lead, off, sub, r):
        parts = []
        for base, per_sub, per_r, cnt in pieces:
            start = pl.multiple_of(off + base + sub * per_sub + r * per_r, cnt)
            idx = (pl.ds(start, cnt), slice(None))
            parts.append(ref[lead + idx] if lead else ref[idx])
        return parts[0] if len(parts) == 1 else jnp.concatenate(parts, axis=0)

    def block(t, carry):
        sub, r = t // dil, t % dil
        qb = rows(q_ref, (0,), 0, sub, r)
        poff = jnp.where(sub > 0, cur, prv)
        psub = jnp.where(sub > 0, sub - 1, nsub - 1)
        kpc = jnp.concatenate([rows(kbuf, (), poff, psub, r), rows(kbuf, (), cur, sub, r)], axis=0)
        vt_c = rows(v_ref, (0,), 0, sub, r).astype(F32).T.astype(BF16)
        tcur = (n % 2) * nblk + t
        tprv = jnp.where(sub > 0, tcur - dil, (1 - n % 2) * nblk + (nsub - 1) * dil + r)
        vtbuf[tcur] = vt_c
        vt_p = vtbuf[tprv]
        no_prev = jnp.logical_and(n == 0, sub == 0)
        bias = jnp.where(no_prev, -jnp.inf, 0.0).astype(F32)
        zero_q = jnp.zeros((blk, LANES), BF16)
        zero_p = jnp.zeros((blk, blk), BF16)
        for h in range(ATT_SLOTS):
            c, e = divmod(h, LANES // hd)
            cs = slice(c * LANES, (c + 1) * LANES)
            q_h = jnp.where(lane_head == e, qb[:, cs], zero_q)
            st = lax.dot_general(kpc[:, cs], q_h, nt, preferred_element_type=F32)
            st_p = st[0:blk] + bias
            s_scr[h] = jnp.where(upper, st_p, st[blk:2 * blk])
            far_scr[h:h + 1, :] = jnp.sum(jnp.where(diag, st_p, 0.0), axis=0, keepdims=True)
        for h in range(ATT_SLOTS):
            sc = s_scr[h]
            far = far_scr[h:h + 1, :]
            m = jnp.maximum(jnp.max(sc, axis=0, keepdims=True), far)
            p = jnp.exp(sc - m)
            p_far = jnp.exp(far - m)
            inv = 1.0 / (jnp.sum(p, axis=0, keepdims=True) + p_far)
            pb = p.astype(BF16)
            p_scr[h, 0:blk] = jnp.where(upper, pb, zero_p)
            p_scr[h, blk:2 * blk] = jnp.where(upper, zero_p, pb)
            far_scr[h:h + 1, :] = p_far
            inv_scr[h:h + 1, :] = inv
            lt_ref[h:h + 1, :] = m - jnp.log(inv)
        o_parts = []
        for h in range(ATT_SLOTS):
            hs = slice(h * hd, (h + 1) * hd)
            vtp_h = vt_p[hs]
            ot = jnp.dot(jnp.concatenate([vtp_h, vt_c[hs]], axis=1), p_scr[h], preferred_element_type=F32)
            o_parts.append((ot + vtp_h.astype(F32) * far_scr[h:h + 1, :]) * inv_scr[h:h + 1, :])
        start = pl.multiple_of(t * blk, blk)
        operm[pl.ds(start, blk), :] = jnp.concatenate(o_parts, axis=0).T
        lperm[pl.ds(start, blk), :] = lt_ref[...].T
        return carry

    lax.fori_loop(0, sb // blk, block, 0)

    nslab = D_ATT // LANES
    for sub in range(nsub):
        for r in range(dil):
            src = (sub * dil + r) * blk
            dst = pl.ds(sub * win + r, blk, stride=dil) if dil > 1 else pl.ds(src, blk)
            for c in range(nslab):
                onat[c, dst, :] = operm[src:src + blk, c * LANES:(c + 1) * LANES]
            l_ref[0, dst, :] = lperm[src:src + blk, :]
    for c in range(nslab):
        o_ref[0, :, c * LANES:(c + 1) * LANES] = onat[c].astype(o_ref.dtype)


def _attn_prompt(u2d, b, s, gi):
    win, dil = ATT_GROUPS[gi]
    sb = ATT_SB
    nsb = s // sb
    assert win // dil == ATT_BLOCK and s == nsb * sb and sb % win == 0
    ncol = N_U // D_ATT
    col0 = (U_ATT + gi * U_TILE) // D_ATT

    def ucol(k):
        return pl.BlockSpec((1, sb, D_ATT), lambda bi, n: (bi, n, col0 + k))

    tok = lambda width: pl.BlockSpec((1, sb, width), lambda bi, n: (bi, n, 0))
    u3 = u2d.reshape(b, s, N_U)
    o, lse = pl.pallas_call(
        functools.partial(_attn_prompt_kernel, win=win, dil=dil),
        grid=(b, nsb),
        in_specs=[ucol(0), ucol(1), ucol(2)],
        out_specs=[tok(D_ATT), tok(LANES)],
        out_shape=[
            jax.ShapeDtypeStruct((b, s, D_ATT), BF16),
            jax.ShapeDtypeStruct((b, s, LANES), F32),
        ],
        scratch_shapes=[
            pltpu.VMEM((2 * sb, D_ATT), BF16), pltpu.VMEM((2 * sb // ATT_BLOCK, D_ATT, ATT_BLOCK), BF16),
            pltpu.VMEM((sb, D_ATT), F32), pltpu.VMEM((D_ATT // LANES, sb, LANES), F32),
            pltpu.VMEM((sb, LANES), F32), pltpu.VMEM((LANES, ATT_BLOCK), F32),
            pltpu.VMEM((ATT_SLOTS, ATT_BLOCK, ATT_BLOCK), F32),
            pltpu.VMEM((ATT_SLOTS, 2 * ATT_BLOCK, ATT_BLOCK), BF16),
            pltpu.VMEM((ATT_SLOTS, ATT_BLOCK), F32), pltpu.VMEM((ATT_SLOTS, ATT_BLOCK), F32),
        ],
        compiler_params=pltpu.CompilerParams(
            dimension_semantics=("parallel", "arbitrary"), vmem_limit_bytes=VMEM_LIMIT),
        name=f"attn_prompt_g{gi}",
    )(u3, u3, u3)
    return o.reshape(b * s, D_ATT), lse.reshape(b * s, LANES)


def _kv_tail_kernel(x_ref, o_ref, nat_ref, *, win, dil):
    span = min(win, IN_TM)
    per_slab = LANES // ATT_HEAD_DIM
    for c in range(D_ATT // LANES):
        cs = slice(c * LANES, (c + 1) * LANES)
        if dil == 1:
            nat = x_ref[0, :, cs].astype(F32)
        else:
            for t in range(win // span):
                for dst, src, cnt in _residue_pieces(win, dil, span):
                    nat_ref[c, pl.ds(t * span + src, cnt, stride=dil), :] = (
                        x_ref[0, t * span + dst:t * span + dst + cnt, cs].astype(F32))
            nat = nat_ref[c]
        o_ref[0, 0, c * per_slab:(c + 1) * per_slab] = nat.T.reshape(per_slab, ATT_HEAD_DIM, win)


def _kv_tail(u2d, b, s, gi):
    win, dil = ATT_GROUPS[gi]
    assert s % win == 0 and (win <= IN_TM or win % IN_TM == 0)
    col0 = (U_ATT + gi * U_TILE) // D_ATT + 1
    last = s // win - 1
    u3 = u2d.reshape(b, s, N_U)
    return pl.pallas_call(
        functools.partial(_kv_tail_kernel, win=win, dil=dil),
        grid=(b, 2),
        in_specs=[pl.BlockSpec((1, win, D_ATT), lambda bi, kv: (bi, last, col0 + kv))],
        out_specs=pl.BlockSpec((1, 1, ATT_SLOTS, ATT_HEAD_DIM, win), lambda bi, kv: (bi, kv, 0, 0, 0)),
        out_shape=jax.ShapeDtypeStruct((b, 2, ATT_SLOTS, ATT_HEAD_DIM, win), F32),
        scratch_shapes=[pltpu.VMEM((D_ATT // LANES, win, LANES), F32)],
        compiler_params=pltpu.CompilerParams(
            dimension_semantics=("parallel", "parallel"), vmem_limit_bytes=VMEM_LIMIT),
        name=f"kv_tail_g{gi}",
    )(u3)


def _attn_decode_kernel(x_ref, c0_ref, c1_ref, c2_ref, o_ref, n0_ref, n1_ref, n2_ref):
    caches = (c0_ref, c1_ref, c2_ref)
    news = (n0_ref, n1_ref, n2_ref)
    ng = N_ATT_GROUPS

    outs_g, lses = [], []
    for g, (win, dil) in enumerate(ATT_GROUPS):
        qc = x_ref[0, 0, g]
        kc = x_ref[0, 0, ng + g]
        vc = x_ref[0, 0, 2 * ng + g]
        kt = caches[g][0, 0, 0]
        vt = caches[g][0, 1, 0]
        lane = lax.broadcasted_iota(jnp.int32, (1, win), 1)
        sc = jnp.sum(kt * qc, axis=0, keepdims=True)
        sc = jnp.where(lane % dil == 0, sc, -jnp.inf)
        sc_new = jnp.sum(kc * qc, axis=0, keepdims=True)
        m = jnp.maximum(jnp.max(sc, axis=1, keepdims=True), sc_new)
        p = jnp.exp(sc - m)
        p_new = jnp.exp(sc_new - m)
        den = jnp.sum(p, axis=1, keepdims=True) + p_new
        outs_g.append((jnp.sum(vt * p, axis=1, keepdims=True) + p_new * vc) / den)
        lses.append(m + jnp.log(den))
        last = lane == win - 1
        news[g][0, 0, 0] = jnp.where(last, kc, pltpu.roll(kt, win - 1, axis=1))
        news[g][0, 1, 0] = jnp.where(last, vc, pltpu.roll(vt, win - 1, axis=1))

    top = jnp.maximum(jnp.maximum(lses[0], lses[1]), lses[2])
    ws = [jnp.exp(l - top) for l in lses]
    o_ref[0, 0] = (ws[0] * outs_g[0] + ws[1] * outs_g[1] + ws[2] * outs_g[2]) / (ws[0] + ws[1] + ws[2])


def _attn_decode(us, caches):
    b = us.shape[0]
    ng = N_ATT_GROUPS
    hd = ATT_HEAD_DIM
    x = us[:, U_ATT:].reshape(b, ng, 3, ATT_SLOTS, hd)
    x = x.transpose(0, 3, 2, 1, 4).reshape(b, ATT_SLOTS, 3 * ng, hd, 1)
    cts = [c.transpose(0, 2, 3, 4, 1) for c in caches]
    cache_specs = []
    for c, (win, dil) in zip(cts, ATT_GROUPS):
        assert c.shape == (b, 2, ATT_SLOTS, hd, win) and win // dil == ATT_BLOCK
        cache_specs.append(pl.BlockSpec((1, 2, 1, hd, win), lambda bi, h: (bi, 0, h, 0, 0)))
    res = pl.pallas_call(
        _attn_decode_kernel,
        grid=(b, ATT_SLOTS),
        in_specs=[pl.BlockSpec((1, 1, 3 * ng, hd, 1), lambda bi, h: (bi, h, 0, 0, 0))] + cache_specs,
        out_specs=[pl.BlockSpec((1, 1, hd, 1), lambda bi, h: (bi, h, 0, 0))] + cache_specs,
        out_shape=[jax.ShapeDtypeStruct((b, ATT_SLOTS, hd, 1), F32)]
        + [jax.ShapeDtypeStruct(c.shape, F32) for c in cts],
        compiler_params=pltpu.CompilerParams(
            dimension_semantics=("parallel", "parallel"), vmem_limit_bytes=VMEM_LIMIT),
        name="attn_decode",
    )(x, *cts)
    o = res[0].reshape(b, D_ATT)
    kvs = [r.transpose(0, 4, 1, 2, 3) for r in res[1:]]
    return o, kvs


def _out_kernel(*refs, n_groups):
    o_refs = refs[:n_groups]
    l_refs = refs[n_groups:2 * n_groups - (n_groups == 1)]
    z_ref, br_ref, gs_ref, ga_ref, x_ref, he_ref, wa_ref, wo_ref, y_ref = refs[len(o_refs) + len(l_refs):]
    if n_groups == 1:
        o = o_refs[0][...].astype(F32)
    else:
        ls = [l[...] for l in l_refs]
        top = functools.reduce(jnp.maximum, ls)
        ws = [jnp.exp(l - top) for l in ls]
        den = functools.reduce(lambda a, c: a + c, ws)
        o = None
        for w, o_ref in zip(ws, o_refs):
            term = _split_dot(w / den, he_ref[...], 2) * o_ref[...].astype(F32)
            o = term if o is None else o + term
    o = o * _silu(z_ref[...].astype(F32))
    br_att = jnp.dot(o.astype(BF16), wa_ref[...], preferred_element_type=F32)
    mix = (_sigmoid(gs_ref[...].astype(F32)) * br_ref[...].astype(F32)
           + _sigmoid(ga_ref[...].astype(F32)) * br_att)
    y_ref[...] = x_ref[...] + jnp.dot(mix.astype(BF16), wo_ref[...], preferred_element_type=F32)


def _out_proj(os_, lses, u2d, br_ssd, x2d, p, tm):
    m = x2d.shape[0]
    row = lambda width, off=0: pl.BlockSpec((tm, width), lambda i: (i, off // width))
    return pl.pallas_call(
        functools.partial(_out_kernel, n_groups=len(os_)),
        grid=(m // tm,),
        in_specs=[row(D_ATT)] * len(os_) + [row(LANES)] * len(lses)
        + [row(D_ATT, U_ZATT), row(D_MODEL), row(D_MODEL, U_GSSD), row(D_MODEL, U_GATT), row(D_MODEL),
           _const_spec((LANES, D_ATT)), _const_spec((D_ATT, D_MODEL)), _const_spec((D_MODEL, D_MODEL))],
        out_specs=row(D_MODEL),
        out_shape=jax.ShapeDtypeStruct((m, D_MODEL), F32),
        compiler_params=pltpu.CompilerParams(
            dimension_semantics=("parallel",), vmem_limit_bytes=VMEM_LIMIT),
        name="out_proj",
    )(*os_, *lses, u2d, br_ssd, u2d, u2d, x2d, p["slot_expand"], p["w_att_br"], p["w_out"])


def _rope_tables(pos):
    half = ATT_HEAD_DIM // 2
    inv = ROPE_THETA ** (-jnp.arange(half, dtype=F32) / half)
    ang = pos.astype(F32)[:, None] * inv[None, :]
    cos, sin = jnp.cos(ang), jnp.sin(ang)
    cos = jnp.concatenate([cos, cos], axis=-1)
    sin = jnp.concatenate([-sin, sin], axis=-1)
    return jnp.tile(cos, (1, LANES // ATT_HEAD_DIM)), jnp.tile(sin, (1, LANES // ATT_HEAD_DIM))


def _storage_positions(s, tm):
    out = []
    for win, dil in ATT_GROUPS:
        idx = np.arange(s)
        if dil > 1:
            for t0 in range(0, s, tm):
                for dst, src, cnt in _residue_pieces(win, dil, tm):
                    idx[t0 + dst:t0 + dst + cnt] = t0 + src + dil * np.arange(cnt)
        out.append(idx)
    return np.stack(out)


def _layer_params(l, norm_w, w_in, conv_w, conv_b, dt_bias, a_log, d_skip, ssd_norm_w, q_norm_w,
                  k_norm_w, w_ssd_br, w_att_br, w_out):
    w = w_in[l]
    o_dt = D_SSD + D_XBC
    o_q = o_dt + SSD_HEADS
    o_k, o_v = o_q + D_ATT_QK, o_q + 2 * D_ATT_QK
    o_zatt = o_q + 3 * D_ATT_QK
    o_gssd, o_gatt = o_zatt + D_ATT, o_zatt + D_ATT + D_MODEL
    cols = [w[:, :o_dt], w[:, o_gssd:o_gatt], w[:, o_gatt:o_gatt + D_MODEL], w[:, o_zatt:o_gssd]]
    for g in range(N_ATT_GROUPS):
        cols += [w[:, o + g * D_ATT:o + (g + 1) * D_ATT] for o in (o_q, o_k, o_v)]
    pad = LANES - SSD_HEADS
    lane_i = np.arange(LANES)
    return dict(
        norm_w=norm_w[l][None, :],
        w_main=jnp.concatenate(cols, axis=1).astype(BF16),
        w_dt=jnp.pad(w[:, o_dt:o_q], ((0, 0), (0, pad))).astype(BF16),
        conv_w=conv_w[l], conv_b=conv_b[l][None, :],
        dt_bias=jnp.pad(dt_bias[l].astype(F32), (0, pad))[None, :],
        a_neg=jnp.pad(-jnp.exp(a_log[l].astype(F32)), (0, pad))[None, :],
        d_skip=jnp.repeat(d_skip[l].astype(F32), SSD_HEAD_DIM)[None, :],
        ssd_norm_w=ssd_norm_w[l][None, :],
        head_expand=jnp.asarray(
            lane_i[:, None] == (np.arange(D_SSD) // SSD_HEAD_DIM)[None, :], BF16),
        slot_expand=jnp.asarray(
            lane_i[:, None] == (np.arange(D_ATT) // ATT_HEAD_DIM)[None, :], BF16),
        head_seg=jnp.asarray(
            lane_i[:, None] // ATT_HEAD_DIM == lane_i[None, :] // ATT_HEAD_DIM, BF16),
        qk_norm_w=jnp.concatenate([jnp.tile(q_norm_w[l].astype(F32) * ATT_SCALE, ATT_SLOTS),
                                   jnp.tile(k_norm_w[l].astype(F32), ATT_SLOTS)])[None, :],
        w_ssd_br=w_ssd_br[l].astype(BF16), w_att_br=w_att_br[l].astype(BF16),
        w_out=w_out[l].astype(BF16),
    )


def _prompt_layer(x, p):
    b, s, _ = x.shape
    assert s % ATT_SB == 0 and ATT_SB % IN_TM == 0
    x2d = x.reshape(b * s, D_MODEL)
    cos, sin = _rope_tables(jnp.arange(s, dtype=jnp.int32))
    order = _storage_positions(s, IN_TM)
    u, dt = _inproj(x2d, p, cos[order], sin[order], BF16, IN_TM, True)
    conv0 = jnp.zeros((b, CONV_WIDTH - 1, D_XBC), F32)
    ssm0 = jnp.zeros((b, SSD_HEADS, SSD_HEAD_DIM, SSD_STATE), F32)
    br_ssd, conv_new, ssm_new = _ssd(u.reshape(b, s, N_U), dt.reshape(b, s, LANES), conv0, ssm0, p, BF16)
    os_, lses, kvs = [], [], []
    for gi in range(N_ATT_GROUPS):
        o, lse = _attn_prompt(u, b, s, gi)
        os_.append(o)
        lses.append(lse)
        kvs.append(_kv_tail(u, b, s, gi).transpose(0, 4, 1, 2, 3))
    y = _out_proj(os_, lses, u, br_ssd.reshape(b * s, D_MODEL), x2d, p, 512)
    return y.reshape(b, s, D_MODEL), conv_new, ssm_new, kvs


def _decode_layer(x, conv_state, ssm_state, caches, p):
    b, s, _ = x.shape
    assert s == 1
    x2d = x.reshape(b, D_MODEL)
    cos, sin = _rope_tables(jnp.full((b,), PAST_LEN, jnp.int32))
    cos3 = jnp.broadcast_to(cos[None], (N_ATT_GROUPS, b, LANES))
    sin3 = jnp.broadcast_to(sin[None], (N_ATT_GROUPS, b, LANES))
    u, dt = _inproj(x2d, p, cos3, sin3, F32, b, False)
    br_ssd, conv_new, ssm_new = _ssd(u.reshape(b, s, N_U), dt.reshape(b, s, LANES),
                                     conv_state, ssm_state, p, F32)
    o, kvs = _attn_decode(u, caches)
    y = _out_proj([o], [], u, br_ssd.reshape(b, D_MODEL), x2d, p, b)
    return y.reshape(b, s, D_MODEL), conv_new, ssm_new, kvs


def kernel(x_prompt, x_sample, state_conv, state_ssm, cache_kv_w128, cache_kv_w512, cache_kv_w2048,
           norm_w, w_in, conv_w, conv_b, dt_bias, a_log, d_skip, ssd_norm_w, q_norm_w, k_norm_w,
           w_ssd_br, w_att_br, w_out):
    yp, ys = x_prompt, x_sample
    outs_p = [[] for _ in range(5)]
    outs_s = [[] for _ in range(5)]
    for l in range(norm_w.shape[0]):
        p = _layer_params(l, norm_w, w_in, conv_w, conv_b, dt_bias, a_log, d_skip, ssd_norm_w,
                          q_norm_w, k_norm_w, w_ssd_br, w_att_br, w_out)
        yp, c, h, kv = _prompt_layer(yp, p)
        for acc, val in zip(outs_p, [c, h] + kv):
            acc.append(val)
        ys, c, h, kv = _decode_layer(ys, state_conv[l], state_ssm[l],
                                     [cache_kv_w128[l], cache_kv_w512[l], cache_kv_w2048[l]], p)
        for acc, val in zip(outs_s, [c, h] + kv):
            acc.append(val)
    return (yp, ys, *[jnp.stack(a) for a in outs_p], *[jnp.stack(a) for a in outs_s])
```

```python
import functools

import numpy as np
import jax
import jax.numpy as jnp
from jax import lax
from jax.experimental import pallas as pl
from jax.experimental.pallas import tpu as pltpu

F32 = jnp.float32
BF16 = jnp.bfloat16

D_MODEL = 1024
D_SSD = 2048
SSD_HEADS = 32
SSD_HEAD_DIM = 64
SSD_GROUPS = 4
SSD_STATE = 128
CONV_WIDTH = 4
D_BC = SSD_GROUPS * SSD_STATE
D_XBC = D_SSD + 2 * D_BC
ATT_HEAD_DIM = 64
ATT_SLOTS = 8
ATT_GROUPS = ((128, 1), (512, 4), (2048, 16))
N_ATT_GROUPS = len(ATT_GROUPS)
D_ATT = ATT_SLOTS * ATT_HEAD_DIM
D_ATT_QK = N_ATT_GROUPS * D_ATT
ATT_SCALE = ATT_HEAD_DIM ** -0.5
ROPE_THETA = 10000.0
EPS = 1e-6
PAST_LEN = 16384

U_Z, U_X, U_B, U_C, U_GSSD, U_GATT, U_ZATT = 0, 2048, 4096, 4608, 5120, 6144, 7168
U_ATT = 7680
U_TILE = 3 * D_ATT
N_U = U_ATT + N_ATT_GROUPS * U_TILE
N_PLAIN_TILES = U_ATT // U_TILE

LANES = 128
SUBLANES = 8
SSD_CHUNK = 128
SSD_CHUNKS_PER_STEP = 4
ATT_BLOCK = 128
ATT_SB = 2048
IN_TM = 1024
DEC_HEADS_PER_STEP = 4
VMEM_LIMIT = 56 * 1024 * 1024


def _split_dot(a, m, terms, left=False):
    out = None
    r = a
    for t in range(terms):
        p = r.astype(BF16)
        d = (jnp.dot(m, p, preferred_element_type=F32) if left
             else jnp.dot(p, m, preferred_element_type=F32))
        out = d if out is None else out + d
        if t + 1 < terms:
            r = r - p.astype(F32)
    return out


def _sigmoid(x):
    return 1.0 / (1.0 + jnp.exp(-x))


def _silu(x):
    return x * _sigmoid(x)


def _softplus(x):
    return jnp.maximum(x, 0.0) + jnp.log1p(jnp.exp(-jnp.abs(x)))


def _const_spec(shape):
    return pl.BlockSpec(shape, lambda *_: (0,) * len(shape))


def _residue_pieces(win, dil, tm):
    span = min(win, tm)
    per = span // dil
    return [(w * span + r * per, w * span + r, per)
            for w in range(max(tm // win, 1)) for r in range(dil)]


def _inproj_kernel(x_ref, nw_ref, w_ref, wdt_ref, cos_ref, sin_ref, qkw_ref, seg_ref,
                   u_ref, dt_ref, hn_ref, *perm_refs, tm, permute):
    j = pl.program_id(1)

    @pl.when(j == 0)
    def _():
        x = x_ref[...]
        ms = jnp.mean(x * x, axis=-1, keepdims=True)
        hn = x * lax.rsqrt(ms + EPS) * nw_ref[...]
        hn_ref[0] = hn.astype(BF16)
        dt_ref[...] = jnp.dot(hn_ref[0], wdt_ref[...], preferred_element_type=F32)
        if permute:
            (hnf_ref,) = perm_refs
            for c in range(D_MODEL // LANES):
                hnf_ref[c] = hn[:, c * LANES:(c + 1) * LANES]
            for g, (win, dil) in enumerate(ATT_GROUPS):
                if dil == 1:
                    continue
                for dst, src, cnt in _residue_pieces(win, dil, tm):
                    for c in range(D_MODEL // LANES):
                        hn_ref[g, dst:dst + cnt, c * LANES:(c + 1) * LANES] = (
                            hnf_ref[c, pl.ds(src, cnt, stride=dil), :].astype(BF16))

    @pl.when(j < N_PLAIN_TILES)
    def _():
        u_ref[...] = jnp.dot(hn_ref[0], w_ref[...], preferred_element_type=F32).astype(u_ref.dtype)

    half = ATT_HEAD_DIM // 2
    for g, (_, dil) in enumerate(ATT_GROUPS):
        @pl.when(j == N_PLAIN_TILES + g)
        def _(g=g, dil=dil):
            src = g if (permute and dil > 1) else 0
            acc = jnp.dot(hn_ref[src], w_ref[...], preferred_element_type=F32)
            cos, sin, seg = cos_ref[0], sin_ref[0], seg_ref[...]
            lane = lax.broadcasted_iota(jnp.int32, (tm, LANES), 1)
            first = lane % ATT_HEAD_DIM < half
            for c in range(2 * D_ATT // LANES):
                cs = slice(c * LANES, (c + 1) * LANES)
                xs = acc[:, cs]
                ss = _split_dot(xs * xs, seg, 2)
                xn = xs * lax.rsqrt(ss * (1.0 / ATT_HEAD_DIM) + EPS) * qkw_ref[:, cs]
                partner = jnp.where(first, pltpu.roll(xn, LANES - half, axis=1),
                                    pltpu.roll(xn, half, axis=1))
                u_ref[:, cs] = (xn * cos + partner * sin).astype(u_ref.dtype)
            u_ref[:, 2 * D_ATT:] = acc[:, 2 * D_ATT:].astype(u_ref.dtype)


def _inproj(x2d, p, cos3, sin3, out_dtype, tm, permute):
    m = x2d.shape[0]
    period = cos3.shape[1] // tm
    tab = pl.BlockSpec((1, tm, LANES),
                       lambda i, j: (jnp.clip(j - N_PLAIN_TILES, 0, N_ATT_GROUPS - 1), i % period, 0))
    scratch = [pltpu.VMEM((N_ATT_GROUPS if permute else 1, tm, D_MODEL), BF16)]
    if permute:
        scratch.append(pltpu.VMEM((D_MODEL // LANES, tm, LANES), F32))
    return pl.pallas_call(
        functools.partial(_inproj_kernel, tm=tm, permute=permute),
        grid=(m // tm, N_U // U_TILE),
        in_specs=[
            pl.BlockSpec((tm, D_MODEL), lambda i, j: (i, 0)),
            _const_spec((1, D_MODEL)),
            pl.BlockSpec((D_MODEL, U_TILE), lambda i, j: (0, j)),
            _const_spec((D_MODEL, LANES)),
            tab, tab,
            _const_spec((1, 2 * D_ATT)),
            _const_spec((LANES, LANES)),
        ],
        out_specs=[
            pl.BlockSpec((tm, U_TILE), lambda i, j: (i, j)),
            pl.BlockSpec((tm, LANES), lambda i, j: (i, 0)),
        ],
        out_shape=[
            jax.ShapeDtypeStruct((m, N_U), out_dtype),
            jax.ShapeDtypeStruct((m, LANES), F32),
        ],
        scratch_shapes=scratch,
        compiler_params=pltpu.CompilerParams(
            dimension_semantics=("parallel", "arbitrary"), vmem_limit_bytes=VMEM_LIMIT),
        name="inproj",
    )(x2d, p["norm_w"], p["w_main"], p["w_dt"], cos3, sin3, p["qk_norm_w"], p["head_seg"])


def _ssd_kernel(z_ref, x_ref, bm_ref, cm_ref, dt_ref, cs_ref, h0_ref, cw_ref, cb_ref, dtb_ref,
                a_ref, dsk_ref, nw_ref, e_ref, sh_ref, wbr_ref, y_ref, cso_ref, ho_ref,
                xin_ref, dtin_ref, ht_ref, yd_ref, *pad_refs, q, t, nc, nsub):
    c = pl.program_id(1)
    tail = CONV_WIDTH - 1
    base = SUBLANES
    hp = D_SSD
    gw = D_SSD // SSD_GROUPS
    e_per_g = SSD_HEADS // SSD_GROUPS
    full = t == q

    @pl.when(c == 0)
    def _():
        xin_ref[0:2 * base if full else base, :] = jnp.zeros((2 * base if full else base, D_XBC), F32)
        xin_ref[base - tail:base, :] = cs_ref[0]
        ht_ref[...] = h0_ref[0].reshape(hp, SSD_STATE).T

    if not full:
        xin_ref[base:base + q, :] = jnp.zeros((q, D_XBC), F32)
        dtin_ref[...] = jnp.zeros((q, LANES), F32)
        (zin_ref,) = pad_refs
        zin_ref[...] = jnp.zeros((q, D_SSD), F32)
        zin_ref[0:t, :] = z_ref[0].astype(F32)
        xin_ref[base:base + t, 0:D_SSD] = x_ref[0].astype(F32)
        xin_ref[base:base + t, D_SSD:D_SSD + D_BC] = bm_ref[0].astype(F32)
        xin_ref[base:base + t, D_SSD + D_BC:D_XBC] = cm_ref[0].astype(F32)
        dtin_ref[0:t, :] = dt_ref[0]

    def conv(src_ref, r0, off, lo, hi):
        if not full:
            acc = cb_ref[:, lo:hi]
            for j in range(CONV_WIDTH):
                acc = acc + cw_ref[j:j + 1, lo:hi] * xin_ref[base - tail + j:base - tail + j + q, lo:hi]
            return _silu(acc)
        xb = src_ref[0, r0:r0 + q, off:off + hi - lo]
        xf = xb.astype(F32)
        shifted = jnp.dot(sh_ref[...], xb, preferred_element_type=F32)
        acc = cb_ref[:, lo:hi] + cw_ref[tail:tail + 1, lo:hi] * xf
        head = acc[0:base]
        for j in range(tail):
            acc = acc + cw_ref[j:j + 1, lo:hi] * shifted[j * q:(j + 1) * q]
            head = head + cw_ref[j:j + 1, lo:hi] * (
                shifted[j * q:j * q + base] + xin_ref[base - tail + j:2 * base - tail + j, lo:hi])
        xin_ref[0:base, lo:hi] = xf[q - base:q]
        return _silu(jnp.concatenate([head, acc[base:]], axis=0))

    ri = lax.broadcasted_iota(jnp.int32, (q, q), 0)
    ci = lax.broadcasted_iota(jnp.int32, (q, q), 1)
    causal = ri >= ci
    tril = jnp.where(causal, 1.0, 0.0).astype(BF16)
    e_mat = e_ref[...]

    def chunk(sub):
        r0 = sub * q
        dt = _softplus((dt_ref[0, r0:r0 + q, :] if full else dtin_ref[...]) + dtb_ref[...])
        if not full:
            rows = lax.broadcasted_iota(jnp.int32, (q, LANES), 0)
            dt = jnp.where(rows < t, dt, 0.0)
        a = dt * a_ref[...]
        acum = _split_dot(a, tril, 3, left=True)
        acum_t = acum.T
        a_last = acum[q - 1:q, :]
        dt_e = _split_dot(dt, e_mat, 2)
        ea_e = _split_dot(jnp.exp(acum), e_mat, 2)
        ds_e = _split_dot(jnp.exp(a_last - acum), e_mat, 2)

        bmat = conv(bm_ref, r0, 0, D_SSD, D_SSD + D_BC)
        cmat = conv(cm_ref, r0, 0, D_SSD + D_BC, D_XBC).astype(BF16)
        bt = bmat.T.astype(BF16)

        yn_parts = []
        for g in range(SSD_GROUPS):
            gs = slice(g * gw, (g + 1) * gw)
            xs = conv(x_ref, r0, g * gw, g * gw, (g + 1) * gw)
            xdt = xs * dt_e[:, gs]
            xdt_b = xdt.astype(BF16)
            cg = cmat[:, g * SSD_STATE:(g + 1) * SSD_STATE]
            bgt = bt[g * SSD_STATE:(g + 1) * SSD_STATE, :]
            cb = jnp.dot(cg, bgt, preferred_element_type=F32)
            ht_g = ht_ref[:, gs]
            y_off = jnp.dot(cg, ht_g.astype(BF16), preferred_element_type=F32)
            for e in range(e_per_g):
                h = g * e_per_g + e
                seg = acum[:, h:h + 1] - acum_t[h:h + 1, :]
                lmat = jnp.exp(jnp.where(causal, seg, -jnp.inf))
                mh = (cb * lmat).astype(BF16)
                hs = slice(e * SSD_HEAD_DIM, (e + 1) * SSD_HEAD_DIM)
                yd_ref[sub, :, h * SSD_HEAD_DIM:(h + 1) * SSD_HEAD_DIM] = jnp.dot(
                    mh, xdt_b[:, hs], preferred_element_type=F32)
            y = yd_ref[sub, :, gs] + y_off * ea_e[:, gs] + dsk_ref[:, gs] * xs
            xw = (xdt * ds_e[:, gs]).astype(BF16)
            ht_ref[:, gs] = ht_g * ea_e[q - 1:q, gs] + jnp.dot(bgt, xw, preferred_element_type=F32)
            zg = z_ref[0, r0:r0 + q, gs].astype(F32) if full else zin_ref[:, gs]
            yg = y * _silu(zg)
            ms = jnp.mean(yg * yg, axis=-1, keepdims=True)
            yn_parts.append((yg * lax.rsqrt(ms + EPS) * nw_ref[:, gs]).astype(BF16))
        yn = jnp.concatenate(yn_parts, axis=-1)
        br = jnp.dot(yn, wbr_ref[...], preferred_element_type=F32)
        y_ref[0, r0:r0 + t, :] = br[0:t, :].astype(y_ref.dtype)

    for sub in range(nsub):
        chunk(sub)

    @pl.when(c == nc - 1)
    def _():
        last = base if full else base + t
        cso_ref[0] = xin_ref[last - tail:last, :]
        ho_ref[0] = ht_ref[...].T.reshape(SSD_HEADS, SSD_HEAD_DIM, SSD_STATE)


def _ssd(u3, dt3, conv_state, ssm_state, p, out_dtype):
    b, s, _ = u3.shape
    q = SSD_CHUNK
    t = min(q, s)
    nsub = SSD_CHUNKS_PER_STEP if s % (SSD_CHUNKS_PER_STEP * q) == 0 else 1
    rows = nsub * t
    nc = s // rows
    assert s == nc * rows and (t == q or nc == 1)

    def ucol(width, off):
        return pl.BlockSpec((1, rows, width), lambda bi, ci: (bi, ci, off // width))

    tail = CONV_WIDTH - 1
    ti = np.arange(q)
    shift = np.concatenate([ti[None, :] == (ti[:, None] - (tail - j)) for j in range(tail)], axis=0)
    scratch = [
        pltpu.VMEM((2 * SUBLANES if t == q else SUBLANES + q, D_XBC), F32),
        pltpu.VMEM((q, LANES), F32),
        pltpu.VMEM((SSD_STATE, D_SSD), F32),
        pltpu.VMEM((nsub, q, D_SSD), F32),
    ]
    if t < q:
        scratch.append(pltpu.VMEM((q, D_SSD), F32))
    return pl.pallas_call(
        functools.partial(_ssd_kernel, q=q, t=t, nc=nc, nsub=nsub),
        grid=(b, nc),
        in_specs=[
            ucol(D_SSD, U_Z), ucol(D_SSD, U_X), ucol(D_BC, U_B), ucol(D_BC, U_C),
            pl.BlockSpec((1, rows, LANES), lambda bi, ci: (bi, ci, 0)),
            pl.BlockSpec((1, CONV_WIDTH - 1, D_XBC), lambda bi, ci: (bi, 0, 0)),
            pl.BlockSpec((1, SSD_HEADS, SSD_HEAD_DIM, SSD_STATE), lambda bi, ci: (bi, 0, 0, 0)),
            _const_spec((CONV_WIDTH, D_XBC)), _const_spec((1, D_XBC)),
            _const_spec((1, LANES)), _const_spec((1, LANES)),
            _const_spec((1, D_SSD)), _const_spec((1, D_SSD)),
            _const_spec((LANES, D_SSD)), _const_spec((tail * q, q)), _const_spec((D_SSD, D_MODEL)),
        ],
        out_specs=[
            pl.BlockSpec((1, rows, D_MODEL), lambda bi, ci: (bi, ci, 0)),
            pl.BlockSpec((1, CONV_WIDTH - 1, D_XBC), lambda bi, ci: (bi, 0, 0)),
            pl.BlockSpec((1, SSD_HEADS, SSD_HEAD_DIM, SSD_STATE), lambda bi, ci: (bi, 0, 0, 0)),
        ],
        out_shape=[
            jax.ShapeDtypeStruct((b, s, D_MODEL), out_dtype),
            jax.ShapeDtypeStruct((b, CONV_WIDTH - 1, D_XBC), F32),
            jax.ShapeDtypeStruct((b, SSD_HEADS, SSD_HEAD_DIM, SSD_STATE), F32),
        ],
        scratch_shapes=scratch,
        compiler_params=pltpu.CompilerParams(
            dimension_semantics=("parallel", "arbitrary"), vmem_limit_bytes=VMEM_LIMIT),
        name="ssd",
    )(u3, u3, u3, u3, dt3, conv_state, ssm_state, p["conv_w"], p["conv_b"], p["dt_bias"], p["a_neg"],
      p["d_skip"], p["ssd_norm_w"], p["head_expand"], jnp.asarray(shift, BF16), p["w_ssd_br"])


def _block_pieces(win, dil):
    if win <= IN_TM:
        return [(0, win, ATT_BLOCK, ATT_BLOCK)]
    per = IN_TM // dil
    return [(t * IN_TM, 0, per, per) for t in range(win // IN_TM)]


def _attn_prompt_kernel(q_ref, k_ref, v_ref, o_ref, l_ref, kbuf, vtbuf, operm, onat, lperm, lt_ref,
                        s_scr, p_scr, far_scr, inv_scr, *, win, dil):
    n = pl.program_id(1)
    sb, blk, hd = ATT_SB, ATT_BLOCK, ATT_HEAD_DIM
    nsub = sb // win
    nblk = sb // blk
    pieces = _block_pieces(win, dil)
    cur = pl.multiple_of((n % 2) * sb, sb)
    prv = pl.multiple_of(sb - (n % 2) * sb, sb)
    kbuf[pl.ds(cur, sb), :] = k_ref[0]

    @pl.when(n == 0)
    def _():
        kbuf[pl.ds(prv, sb), :] = jnp.zeros((sb, D_ATT), BF16)
        vtbuf[pl.ds(nblk, nblk)] = jnp.zeros((nblk, D_ATT, blk), BF16)

    lt_ref[...] = jnp.zeros((LANES, blk), F32)
    ki = lax.broadcasted_iota(jnp.int32, (blk, blk), 0)
    qi = lax.broadcasted_iota(jnp.int32, (blk, blk), 1)
    upper = ki > qi
    diag = ki == qi
    lane_head = lax.broadcasted_iota(jnp.int32, (blk, LANES), 1) // hd
    nt = (((1,), (1,)), ((), ()))

    def rows(ref, lead, off, sub, r):
        parts = []
        for base, per_sub, per_r, cnt in pieces:
            start = pl.multiple_of(off + base + sub * per_sub + r * per_r, cnt)
            idx = (pl.ds(start, cnt), slice(None))
            parts.append(ref[lead + idx] if lead else ref[idx])
        return parts[0] if len(parts) == 1 else jnp.concatenate(parts, axis=0)

    def block(t, carry):
        sub, r = t // dil, t % dil
        qb = rows(q_ref, (0,), 0, sub, r)
        poff = jnp.where(sub > 0, cur, prv)
        psub = jnp.where(sub > 0, sub - 1, nsub - 1)
        kpc = jnp.concatenate([rows(kbuf, (), poff, psub, r), rows(kbuf, (), cur, sub, r)], axis=0)
        vt_c = rows(v_ref, (0,), 0, sub, r).astype(F32).T.astype(BF16)
        tcur = (n % 2) * nblk + t
        tprv = jnp.where(sub > 0, tcur - dil, (1 - n % 2) * nblk + (nsub - 1) * dil + r)
        vtbuf[tcur] = vt_c
        vt_p = vtbuf[tprv]
        no_prev = jnp.logical_and(n == 0, sub == 0)
        bias = jnp.where(no_prev, -jnp.inf, 0.0).astype(F32)
        zero_q = jnp.zeros((blk, LANES), BF16)
        zero_p = jnp.zeros((blk, blk), BF16)
        for h in range(ATT_SLOTS):
            c, e = divmod(h, LANES // hd)
            cs = slice(c * LANES, (c + 1) * LANES)
            q_h = jnp.where(lane_head == e, qb[:, cs], zero_q)
            st = lax.dot_general(kpc[:, cs], q_h, nt, preferred_element_type=F32)
            st_p = st[0:blk] + bias
            s_scr[h] = jnp.where(upper, st_p, st[blk:2 * blk])
            far_scr[h:h + 1, :] = jnp.sum(jnp.where(diag, st_p, 0.0), axis=0, keepdims=True)
        for h in range(ATT_SLOTS):
            sc = s_scr[h]
            far = far_scr[h:h + 1, :]
            m = jnp.maximum(jnp.max(sc, axis=0, keepdims=True), far)
            p = jnp.exp(sc - m)
            p_far = jnp.exp(far - m)
            inv = 1.0 / (jnp.sum(p, axis=0, keepdims=True) + p_far)
            pb = p.astype(BF16)
            p_scr[h, 0:blk] = jnp.where(upper, pb, zero_p)
            p_scr[h, blk:2 * blk] = jnp.where(upper, zero_p, pb)
            far_scr[h:h + 1, :] = p_far
            inv_scr[h:h + 1, :] = inv
            lt_ref[h:h + 1, :] = m - jnp.log(inv)
        o_parts = []
        for h in range(ATT_SLOTS):
            hs = slice(h * hd, (h + 1) * hd)
            vtp_h = vt_p[hs]
            ot = jnp.dot(jnp.concatenate([vtp_h, vt_c[hs]], axis=1), p_scr[h], preferred_element_type=F32)
            o_parts.append((ot + vtp_h.astype(F32) * far_scr[h:h + 1, :]) * inv_scr[h:h + 1, :])
        start = pl.multiple_of(t * blk, blk)
        operm[pl.ds(start, blk), :] = jnp.concatenate(o_parts, axis=0).T
        lperm[pl.ds(start, blk), :] = lt_ref[...].T
        return carry

    lax.fori_loop(0, sb // blk, block, 0)

    nslab = D_ATT // LANES
    for sub in range(nsub):
        for r in range(dil):
            src = (sub * dil + r) * blk
            dst = pl.ds(sub * win + r, blk, stride=dil) if dil > 1 else pl.ds(src, blk)
            for c in range(nslab):
                onat[c, dst, :] = operm[src:src + blk, c * LANES:(c + 1) * LANES]
            l_ref[0, dst, :] = lperm[src:src + blk, :]
    for c in range(nslab):
        o_ref[0, :, c * LANES:(c + 1) * LANES] = onat[c].astype(o_ref.dtype)


def _attn_prompt(u2d, b, s, gi):
    win, dil = ATT_GROUPS[gi]
    sb = ATT_SB
    nsb = s // sb
    assert win // dil == ATT_BLOCK and s == nsb * sb and sb % win == 0
    col0 = (U_ATT + gi * U_TILE) // D_ATT

    def ucol(k):
        return pl.BlockSpec((1, sb, D_ATT), lambda bi, n: (bi, n, col0 + k))

    tok = lambda width: pl.BlockSpec((1, sb, width), lambda bi, n: (bi, n, 0))
    u3 = u2d.reshape(b, s, N_U)
    o, lse = pl.pallas_call(
        functools.partial(_attn_prompt_kernel, win=win, dil=dil),
        grid=(b, nsb),
        in_specs=[ucol(0), ucol(1), ucol(2)],
        out_specs=[tok(D_ATT), tok(LANES)],
        out_shape=[
            jax.ShapeDtypeStruct((b, s, D_ATT), BF16),
            jax.ShapeDtypeStruct((b, s, LANES), F32),
        ],
        scratch_shapes=[
            pltpu.VMEM((2 * sb, D_ATT), BF16), pltpu.VMEM((2 * sb // ATT_BLOCK, D_ATT, ATT_BLOCK), BF16),
            pltpu.VMEM((sb, D_ATT), F32), pltpu.VMEM((D_ATT // LANES, sb, LANES), F32),
            pltpu.VMEM((sb, LANES), F32), pltpu.VMEM((LANES, ATT_BLOCK), F32),
            pltpu.VMEM((ATT_SLOTS, ATT_BLOCK, ATT_BLOCK), F32),
            pltpu.VMEM((ATT_SLOTS, 2 * ATT_BLOCK, ATT_BLOCK), BF16),
            pltpu.VMEM((ATT_SLOTS, ATT_BLOCK), F32), pltpu.VMEM((ATT_SLOTS, ATT_BLOCK), F32),
        ],
        compiler_params=pltpu.CompilerParams(
            dimension_semantics=("parallel", "arbitrary"), vmem_limit_bytes=VMEM_LIMIT),
        name=f"attn_prompt_g{gi}",
    )(u3, u3, u3)
    return o.reshape(b * s, D_ATT), lse.reshape(b * s, LANES)


def _kv_tail_kernel(x_ref, o_ref, nat_ref, *, win, dil):
    span = min(win, IN_TM)
    per_slab = LANES // ATT_HEAD_DIM
    for c in range(D_ATT // LANES):
        cs = slice(c * LANES, (c + 1) * LANES)
        if dil == 1:
            nat = x_ref[0, :, cs].astype(F32)
        else:
            for t in range(win // span):
                for dst, src, cnt in _residue_pieces(win, dil, span):
                    nat_ref[c, pl.ds(t * span + src, cnt, stride=dil), :] = (
                        x_ref[0, t * span + dst:t * span + dst + cnt, cs].astype(F32))
            nat = nat_ref[c]
        o_ref[0, 0, c * per_slab:(c + 1) * per_slab] = nat.T.reshape(per_slab, ATT_HEAD_DIM, win)


def _kv_tail(u2d, b, s, gi):
    win, dil = ATT_GROUPS[gi]
    assert s % win == 0 and (win <= IN_TM or win % IN_TM == 0)
    col0 = (U_ATT + gi * U_TILE) // D_ATT + 1
    last = s // win - 1
    u3 = u2d.reshape(b, s, N_U)
    return pl.pallas_call(
        functools.partial(_kv_tail_kernel, win=win, dil=dil),
        grid=(b, 2),
        in_specs=[pl.BlockSpec((1, win, D_ATT), lambda bi, kv: (bi, last, col0 + kv))],
        out_specs=pl.BlockSpec((1, 1, ATT_SLOTS, ATT_HEAD_DIM, win), lambda bi, kv: (bi, kv, 0, 0, 0)),
        out_shape=jax.ShapeDtypeStruct((b, 2, ATT_SLOTS, ATT_HEAD_DIM, win), F32),
        scratch_shapes=[pltpu.VMEM((D_ATT // LANES, win, LANES), F32)],
        compiler_params=pltpu.CompilerParams(
            dimension_semantics=("parallel", "parallel"), vmem_limit_bytes=VMEM_LIMIT),
        name=f"kv_tail_g{gi}",
    )(u3)


def _attn_decode_kernel(x_ref, c0_ref, c1_ref, c2_ref, o_ref, n0_ref, n1_ref, n2_ref):
    caches = (c0_ref, c1_ref, c2_ref)
    news = (n0_ref, n1_ref, n2_ref)
    ng = N_ATT_GROUPS

    for hh in range(x_ref.shape[1]):
        outs_g, lses = [], []
        for g, (win, dil) in enumerate(ATT_GROUPS):
            qc = x_ref[0, hh, g]
            kc = x_ref[0, hh, ng + g]
            vc = x_ref[0, hh, 2 * ng + g]
            kt = caches[g][0, 0, hh]
            vt = caches[g][0, 1, hh]
            lane = lax.broadcasted_iota(jnp.int32, (1, win), 1)
            sc = jnp.sum(kt * qc, axis=0, keepdims=True)
            sc = jnp.where(lane % dil == 0, sc, -jnp.inf)
            sc_new = jnp.sum(kc * qc, axis=0, keepdims=True)
            m = jnp.maximum(jnp.max(sc, axis=1, keepdims=True), sc_new)
            p = jnp.exp(sc - m)
            p_new = jnp.exp(sc_new - m)
            den = jnp.sum(p, axis=1, keepdims=True) + p_new
            outs_g.append((jnp.sum(vt * p, axis=1, keepdims=True) + p_new * vc) / den)
            lses.append(m + jnp.log(den))
            last = lane == win - 1
            news[g][0, 0, hh] = jnp.where(last, kc, pltpu.roll(kt, win - 1, axis=1))
            news[g][0, 1, hh] = jnp.where(last, vc, pltpu.roll(vt, win - 1, axis=1))

        top = jnp.maximum(jnp.maximum(lses[0], lses[1]), lses[2])
        ws = [jnp.exp(l - top) for l in lses]
        o_ref[0, hh] = ((ws[0] * outs_g[0] + ws[1] * outs_g[1] + ws[2] * outs_g[2])
                        / (ws[0] + ws[1] + ws[2]))


def _attn_decode(us, caches):
    b = us.shape[0]
    ng = N_ATT_GROUPS
    hd = ATT_HEAD_DIM
    hb = DEC_HEADS_PER_STEP
    x = us[:, U_ATT:].reshape(b, ng, 3, ATT_SLOTS, hd)
    x = x.transpose(0, 3, 2, 1, 4).reshape(b, ATT_SLOTS, 3 * ng, hd, 1)
    cts = [c.transpose(0, 2, 3, 4, 1) for c in caches]
    cache_specs = []
    for c, (win, dil) in zip(cts, ATT_GROUPS):
        assert c.shape == (b, 2, ATT_SLOTS, hd, win) and win // dil == ATT_BLOCK
        cache_specs.append(pl.BlockSpec((1, 2, hb, hd, win), lambda bi, h: (bi, 0, h, 0, 0)))
    res = pl.pallas_call(
        _attn_decode_kernel,
        grid=(b, ATT_SLOTS // hb),
        in_specs=[pl.BlockSpec((1, hb, 3 * ng, hd, 1), lambda bi, h: (bi, h, 0, 0, 0))] + cache_specs,
        out_specs=[pl.BlockSpec((1, hb, hd, 1), lambda bi, h: (bi, h, 0, 0))] + cache_specs,
        out_shape=[jax.ShapeDtypeStruct((b, ATT_SLOTS, hd, 1), F32)]
        + [jax.ShapeDtypeStruct(c.shape, F32) for c in cts],
        compiler_params=pltpu.CompilerParams(
            dimension_semantics=("parallel", "parallel"), vmem_limit_bytes=VMEM_LIMIT),
        name="attn_decode",
    )(x, *cts)
    o = res[0].reshape(b, D_ATT)
    kvs = [r.transpose(0, 4, 1, 2, 3) for r in res[1:]]
    return o, kvs


def _out_kernel(*refs, n_groups):
    o_refs = refs[:n_groups]
    l_refs = refs[n_groups:2 * n_groups - (n_groups == 1)]
    z_ref, br_ref, gs_ref, ga_ref, x_ref, he_ref, wa_ref, wo_ref, y_ref = refs[len(o_refs) + len(l_refs):]
    if n_groups == 1:
        o = o_refs[0][...].astype(F32)
    else:
        ls = [l[...] for l in l_refs]
        top = functools.reduce(jnp.maximum, ls)
        ws = [jnp.exp(l - top) for l in ls]
        den = functools.reduce(lambda a, c: a + c, ws)
        o = None
        for w, o_ref in zip(ws, o_refs):
            term = _split_dot(w / den, he_ref[...], 2) * o_ref[...].astype(F32)
            o = term if o is None else o + term
    o = o * _silu(z_ref[...].astype(F32))
    br_att = jnp.dot(o.astype(BF16), wa_ref[...], preferred_element_type=F32)
    mix = (_sigmoid(gs_ref[...].astype(F32)) * br_ref[...].astype(F32)
           + _sigmoid(ga_ref[...].astype(F32)) * br_att)
    y_ref[...] = x_ref[...] + jnp.dot(mix.astype(BF16), wo_ref[...], preferred_element_type=F32)


def _out_proj(os_, lses, u2d, br_ssd, x2d, p, tm):
    m = x2d.shape[0]
    row = lambda width, off=0: pl.BlockSpec((tm, width), lambda i: (i, off // width))
    return pl.pallas_call(
        functools.partial(_out_kernel, n_groups=len(os_)),
        grid=(m // tm,),
        in_specs=[row(D_ATT)] * len(os_) + [row(LANES)] * len(lses)
        + [row(D_ATT, U_ZATT), row(D_MODEL), row(D_MODEL, U_GSSD), row(D_MODEL, U_GATT), row(D_MODEL),
           _const_spec((LANES, D_ATT)), _const_spec((D_ATT, D_MODEL)), _const_spec((D_MODEL, D_MODEL))],
        out_specs=row(D_MODEL),
        out_shape=jax.ShapeDtypeStruct((m, D_MODEL), F32),
        compiler_params=pltpu.CompilerParams(
            dimension_semantics=("parallel",), vmem_limit_bytes=VMEM_LIMIT),
        name="out_proj",
    )(*os_, *lses, u2d, br_ssd, u2d, u2d, x2d, p["slot_expand"], p["w_att_br"], p["w_out"])


def _rope_tables(pos):
    half = ATT_HEAD_DIM // 2
    inv = ROPE_THETA ** (-jnp.arange(half, dtype=F32) / half)
    ang = pos.astype(F32)[:, None] * inv[None, :]
    cos, sin = jnp.cos(ang), jnp.sin(ang)
    cos = jnp.concatenate([cos, cos], axis=-1)
    sin = jnp.concatenate([-sin, sin], axis=-1)
    return jnp.tile(cos, (1, LANES // ATT_HEAD_DIM)), jnp.tile(sin, (1, LANES // ATT_HEAD_DIM))


def _storage_positions(s, tm):
    out = []
    for win, dil in ATT_GROUPS:
        idx = np.arange(s)
        if dil > 1:
            for t0 in range(0, s, tm):
                for dst, src, cnt in _residue_pieces(win, dil, tm):
                    idx[t0 + dst:t0 + dst + cnt] = t0 + src + dil * np.arange(cnt)
        out.append(idx)
    return np.stack(out)


def _layer_params(l, norm_w, w_in, conv_w, conv_b, dt_bias, a_log, d_skip, ssd_norm_w, q_norm_w,
                  k_norm_w, w_ssd_br, w_att_br, w_out):
    w = w_in[l]
    o_dt = D_SSD + D_XBC
    o_q = o_dt + SSD_HEADS
    o_k, o_v = o_q + D_ATT_QK, o_q + 2 * D_ATT_QK
    o_zatt = o_q + 3 * D_ATT_QK
    o_gssd, o_gatt = o_zatt + D_ATT, o_zatt + D_ATT + D_MODEL
    cols = [w[:, :o_dt], w[:, o_gssd:o_gatt], w[:, o_gatt:o_gatt + D_MODEL], w[:, o_zatt:o_gssd]]
    for g in range(N_ATT_GROUPS):
        cols += [w[:, o + g * D_ATT:o + (g + 1) * D_ATT] for o in (o_q, o_k, o_v)]
    pad = LANES - SSD_HEADS
    lane_i = np.arange(LANES)
    return dict(
        norm_w=norm_w[l][None, :],
        w_main=jnp.concatenate(cols, axis=1).astype(BF16),
        w_dt=jnp.pad(w[:, o_dt:o_q], ((0, 0), (0, pad))).astype(BF16),
        conv_w=conv_w[l], conv_b=conv_b[l][None, :],
        dt_bias=jnp.pad(dt_bias[l].astype(F32), (0, pad))[None, :],
        a_neg=jnp.pad(-jnp.exp(a_log[l].astype(F32)), (0, pad))[None, :],
        d_skip=jnp.repeat(d_skip[l].astype(F32), SSD_HEAD_DIM)[None, :],
        ssd_norm_w=ssd_norm_w[l][None, :],
        head_expand=jnp.asarray(
            lane_i[:, None] == (np.arange(D_SSD) // SSD_HEAD_DIM)[None, :], BF16),
        slot_expand=jnp.asarray(
            lane_i[:, None] == (np.arange(D_ATT) // ATT_HEAD_DIM)[None, :], BF16),
        head_seg=jnp.asarray(
            lane_i[:, None] // ATT_HEAD_DIM == lane_i[None, :] // ATT_HEAD_DIM, BF16),
        qk_norm_w=jnp.concatenate([jnp.tile(q_norm_w[l].astype(F32) * ATT_SCALE, ATT_SLOTS),
                                   jnp.tile(k_norm_w[l].astype(F32), ATT_SLOTS)])[None, :],
        w_ssd_br=w_ssd_br[l].astype(BF16), w_att_br=w_att_br[l].astype(BF16),
        w_out=w_out[l].astype(BF16),
    )


def _prompt_layer(x, p):
    b, s, _ = x.shape
    assert s % ATT_SB == 0 and ATT_SB % IN_TM == 0
    x2d = x.reshape(b * s, D_MODEL)
    order = jnp.asarray(_storage_positions(s, IN_TM).reshape(-1), jnp.int32)
    cos, sin = _rope_tables(order)
    shape3 = (N_ATT_GROUPS, s, LANES)
    u, dt = _inproj(x2d, p, cos.reshape(shape3), sin.reshape(shape3), BF16, IN_TM, True)
    conv0 = jnp.zeros((b, CONV_WIDTH - 1, D_XBC), F32)
    ssm0 = jnp.zeros((b, SSD_HEADS, SSD_HEAD_DIM, SSD_STATE), F32)
    br_ssd, conv_new, ssm_new = _ssd(u.reshape(b, s, N_U), dt.reshape(b, s, LANES), conv0, ssm0, p, BF16)
    os_, lses, kvs = [], [], []
    for gi in range(N_ATT_GROUPS):
        o, lse = _attn_prompt(u, b, s, gi)
        os_.append(o)
        lses.append(lse)
        kvs.append(_kv_tail(u, b, s, gi).transpose(0, 4, 1, 2, 3))
    y = _out_proj(os_, lses, u, br_ssd.reshape(b * s, D_MODEL), x2d, p, 512)
    return y.reshape(b, s, D_MODEL), conv_new, ssm_new, kvs


def _decode_layer(x, conv_state, ssm_state, caches, p):
    b, s, _ = x.shape
    assert s == 1
    x2d = x.reshape(b, D_MODEL)
    cos, sin = _rope_tables(jnp.full((b,), PAST_LEN, jnp.int32))
    cos3 = jnp.broadcast_to(cos[None], (N_ATT_GROUPS, b, LANES))
    sin3 = jnp.broadcast_to(sin[None], (N_ATT_GROUPS, b, LANES))
    u, dt = _inproj(x2d, p, cos3, sin3, F32, b, False)
    br_ssd, conv_new, ssm_new = _ssd(u.reshape(b, s, N_U), dt.reshape(b, s, LANES),
                                     conv_state, ssm_state, p, F32)
    o, kvs = _attn_decode(u, caches)
    y = _out_proj([o], [], u, br_ssd.reshape(b, D_MODEL), x2d, p, b)
    return y.reshape(b, s, D_MODEL), conv_new, ssm_new, kvs


def kernel(x_prompt, x_sample, state_conv, state_ssm, cache_kv_w128, cache_kv_w512, cache_kv_w2048,
           norm_w, w_in, conv_w, conv_b, dt_bias, a_log, d_skip, ssd_norm_w, q_norm_w, k_norm_w,
           w_ssd_br, w_att_br, w_out):
    yp, ys = x_prompt, x_sample
    outs_p = [[] for _ in range(5)]
    outs_s = [[] for _ in range(5)]
    for l in range(norm_w.shape[0]):
        p = _layer_params(l, norm_w, w_in, conv_w, conv_b, dt_bias, a_log, d_skip, ssd_norm_w,
                          q_norm_w, k_norm_w, w_ssd_br, w_att_br, w_out)
        yp, c, h, kv = _prompt_layer(yp, p)
        for acc, val in zip(outs_p, [c, h] + kv):
            acc.append(val)
        ys, c, h, kv = _decode_layer(ys, state_conv[l], state_ssm[l],
                                     [cache_kv_w128[l], cache_kv_w512[l], cache_kv_w2048[l]], p)
        for acc, val in zip(outs_s, [c, h] + kv):
            acc.append(val)
    return (yp, ys, *[jnp.stack(a) for a in outs_p], *[jnp.stack(a) for a in outs_s])
```

```python
import functools

import numpy as np
import jax
import jax.numpy as jnp
from jax import lax
from jax.experimental import pallas as pl
from jax.experimental.pallas import tpu as pltpu

F32 = jnp.float32
BF16 = jnp.bfloat16

D_MODEL = 1024
D_SSD = 2048
SSD_HEADS = 32
SSD_HEAD_DIM = 64
SSD_GROUPS = 4
SSD_STATE = 128
CONV_WIDTH = 4
D_BC = SSD_GROUPS * SSD_STATE
D_XBC = D_SSD + 2 * D_BC
ATT_HEAD_DIM = 64
ATT_SLOTS = 8
ATT_GROUPS = ((128, 1), (512, 4), (2048, 16))
N_ATT_GROUPS = len(ATT_GROUPS)
D_ATT = ATT_SLOTS * ATT_HEAD_DIM
D_ATT_QK = N_ATT_GROUPS * D_ATT
ATT_SCALE = ATT_HEAD_DIM ** -0.5
ROPE_THETA = 10000.0
EPS = 1e-6
PAST_LEN = 16384

U_Z, U_X, U_B, U_C, U_GSSD, U_GATT, U_ZATT = 0, 2048, 4096, 4608, 5120, 6144, 7168
U_ATT = 7680
U_TILE = 3 * D_ATT
N_U = U_ATT + N_ATT_GROUPS * U_TILE
N_PLAIN_TILES = U_ATT // U_TILE

LANES = 128
SUBLANES = 8
SSD_CHUNK = 128
SSD_CHUNKS_PER_STEP = 4
ATT_BLOCK = 128
ATT_SB = 2048
IN_TM = 1024
DEC_HEADS_PER_STEP = 4
VMEM_LIMIT = 56 * 1024 * 1024


def _split_dot(a, m, terms, left=False):
    out = None
    r = a
    for t in range(terms):
        p = r.astype(BF16)
        d = (jnp.dot(m, p, preferred_element_type=F32) if left
             else jnp.dot(p, m, preferred_element_type=F32))
        out = d if out is None else out + d
        if t + 1 < terms:
            r = r - p.astype(F32)
    return out


def _sigmoid(x):
    return 1.0 / (1.0 + jnp.exp(-x))


def _silu(x):
    return x * _sigmoid(x)


def _softplus(x):
    return jnp.maximum(x, 0.0) + jnp.log1p(jnp.exp(-jnp.abs(x)))


def _const_spec(shape):
    return pl.BlockSpec(shape, lambda *_: (0,) * len(shape))


def _residue_pieces(win, dil, tm):
    span = min(win, tm)
    per = span // dil
    return [(w * span + r * per, w * span + r, per)
            for w in range(max(tm // win, 1)) for r in range(dil)]


def _inproj_kernel(x_ref, nw_ref, w_ref, wdt_ref, cos_ref, sin_ref, qkw_ref, seg_ref,
                   u_ref, dt_ref, hn_ref, *perm_refs, tm, permute):
    j = pl.program_id(1)

    @pl.when(j == 0)
    def _():
        x = x_ref[...]
        ms = jnp.mean(x * x, axis=-1, keepdims=True)
        hn = x * lax.rsqrt(ms + EPS) * nw_ref[...]
        hn_ref[0] = hn.astype(BF16)
        dt_ref[...] = jnp.dot(hn_ref[0], wdt_ref[...], preferred_element_type=F32)
        if permute:
            (hnf_ref,) = perm_refs
            for c in range(D_MODEL // LANES):
                hnf_ref[c] = hn[:, c * LANES:(c + 1) * LANES]
            for g, (win, dil) in enumerate(ATT_GROUPS):
                if dil == 1:
                    continue
                for dst, src, cnt in _residue_pieces(win, dil, tm):
                    for c in range(D_MODEL // LANES):
                        hn_ref[g, dst:dst + cnt, c * LANES:(c + 1) * LANES] = (
                            hnf_ref[c, pl.ds(src, cnt, stride=dil), :].astype(BF16))

    @pl.when(j < N_PLAIN_TILES)
    def _():
        u_ref[...] = jnp.dot(hn_ref[0], w_ref[...], preferred_element_type=F32).astype(u_ref.dtype)

    half = ATT_HEAD_DIM // 2
    for g, (_, dil) in enumerate(ATT_GROUPS):
        @pl.when(j == N_PLAIN_TILES + g)
        def _(g=g, dil=dil):
            src = g if (permute and dil > 1) else 0
            acc = jnp.dot(hn_ref[src], w_ref[...], preferred_element_type=F32)
            cos, sin, seg = cos_ref[0], sin_ref[0], seg_ref[...]
            lane = lax.broadcasted_iota(jnp.int32, (tm, LANES), 1)
            first = lane % ATT_HEAD_DIM < half
            for c in range(2 * D_ATT // LANES):
                cs = slice(c * LANES, (c + 1) * LANES)
                xs = acc[:, cs]
                ss = _split_dot(xs * xs, seg, 2)
                xn = xs * lax.rsqrt(ss * (1.0 / ATT_HEAD_DIM) + EPS) * qkw_ref[:, cs]
                partner = jnp.where(first, pltpu.roll(xn, LANES - half, axis=1),
                                    pltpu.roll(xn, half, axis=1))
                u_ref[:, cs] = (xn * cos + partner * sin).astype(u_ref.dtype)
            u_ref[:, 2 * D_ATT:] = acc[:, 2 * D_ATT:].astype(u_ref.dtype)


def _inproj(x2d, p, cos3, sin3, out_dtype, tm, permute):
    m = x2d.shape[0]
    period = cos3.shape[1] // tm
    tab = pl.BlockSpec((1, tm, LANES),
                       lambda i, j: (jnp.clip(j - N_PLAIN_TILES, 0, N_ATT_GROUPS - 1), i % period, 0))
    scratch = [pltpu.VMEM((N_ATT_GROUPS if permute else 1, tm, D_MODEL), BF16)]
    if permute:
        scratch.append(pltpu.VMEM((D_MODEL // LANES, tm, LANES), F32))
    return pl.pallas_call(
        functools.partial(_inproj_kernel, tm=tm, permute=permute),
        grid=(m // tm, N_U // U_TILE),
        in_specs=[
            pl.BlockSpec((tm, D_MODEL), lambda i, j: (i, 0)),
            _const_spec((1, D_MODEL)),
            pl.BlockSpec((D_MODEL, U_TILE), lambda i, j: (0, j)),
            _const_spec((D_MODEL, LANES)),
            tab, tab,
            _const_spec((1, 2 * D_ATT)),
            _const_spec((LANES, LANES)),
        ],
        out_specs=[
            pl.BlockSpec((tm, U_TILE), lambda i, j: (i, j)),
            pl.BlockSpec((tm, LANES), lambda i, j: (i, 0)),
        ],
        out_shape=[
            jax.ShapeDtypeStruct((m, N_U), out_dtype),
            jax.ShapeDtypeStruct((m, LANES), F32),
        ],
        scratch_shapes=scratch,
        compiler_params=pltpu.CompilerParams(
            dimension_semantics=("parallel", "arbitrary"), vmem_limit_bytes=VMEM_LIMIT),
        name="inproj",
    )(x2d, p["norm_w"], p["w_main"], p["w_dt"], cos3, sin3, p["qk_norm_w"], p["head_seg"])


def _ssd_kernel(z_ref, x_ref, bm_ref, cm_ref, dt_ref, cs_ref, h0_ref, cw_ref, cb_ref, dtb_ref,
                a_ref, dsk_ref, nw_ref, e_ref, sh_ref, wbr_ref, y_ref, cso_ref, ho_ref,
                xin_ref, dtin_ref, ht_ref, yd_ref, *pad_refs, q, t, nc, nsub):
    c = pl.program_id(1)
    tail = CONV_WIDTH - 1
    base = SUBLANES
    hp = D_SSD
    gw = D_SSD // SSD_GROUPS
    e_per_g = SSD_HEADS // SSD_GROUPS
    full = t == q

    @pl.when(c == 0)
    def _():
        xin_ref[0:2 * base if full else base, :] = jnp.zeros((2 * base if full else base, D_XBC), F32)
        xin_ref[base - tail:base, :] = cs_ref[0]
        ht_ref[...] = h0_ref[0].reshape(hp, SSD_STATE).T

    if not full:
        xin_ref[base:base + q, :] = jnp.zeros((q, D_XBC), F32)
        dtin_ref[...] = jnp.zeros((q, LANES), F32)
        (zin_ref,) = pad_refs
        zin_ref[...] = jnp.zeros((q, D_SSD), F32)
        zin_ref[0:t, :] = z_ref[0].astype(F32)
        xin_ref[base:base + t, 0:D_SSD] = x_ref[0].astype(F32)
        xin_ref[base:base + t, D_SSD:D_SSD + D_BC] = bm_ref[0].astype(F32)
        xin_ref[base:base + t, D_SSD + D_BC:D_XBC] = cm_ref[0].astype(F32)
        dtin_ref[0:t, :] = dt_ref[0]

    def conv(src_ref, r0, off, lo, hi):
        if not full:
            acc = cb_ref[:, lo:hi]
            for j in range(CONV_WIDTH):
                acc = acc + cw_ref[j:j + 1, lo:hi] * xin_ref[base - tail + j:base - tail + j + q, lo:hi]
            return _silu(acc)
        xb = src_ref[0, r0:r0 + q, off:off + hi - lo]
        xf = xb.astype(F32)
        shifted = jnp.dot(sh_ref[...], xb, preferred_element_type=F32)
        acc = cb_ref[:, lo:hi] + cw_ref[tail:tail + 1, lo:hi] * xf
        head = acc[0:base]
        for j in range(tail):
            acc = acc + cw_ref[j:j + 1, lo:hi] * shifted[j * q:(j + 1) * q]
            head = head + cw_ref[j:j + 1, lo:hi] * (
                shifted[j * q:j * q + base] + xin_ref[base - tail + j:2 * base - tail + j, lo:hi])
        xin_ref[0:base, lo:hi] = xf[q - base:q]
        return _silu(jnp.concatenate([head, acc[base:]], axis=0))

    ri = lax.broadcasted_iota(jnp.int32, (q, q), 0)
    ci = lax.broadcasted_iota(jnp.int32, (q, q), 1)
    causal = ri >= ci
    tril = jnp.where(causal, 1.0, 0.0).astype(BF16)
    e_mat = e_ref[...]
    heads_per_slab = LANES // SSD_HEAD_DIM
    slab_head = lax.broadcasted_iota(jnp.int32, (q, LANES), 1) // SSD_HEAD_DIM

    def chunk(sub):
        r0 = sub * q
        dt = _softplus((dt_ref[0, r0:r0 + q, :] if full else dtin_ref[...]) + dtb_ref[...])
        if not full:
            rows = lax.broadcasted_iota(jnp.int32, (q, LANES), 0)
            dt = jnp.where(rows < t, dt, 0.0)
        a = dt * a_ref[...]
        acum = _split_dot(a, tril, 3, left=True)
        acum_t = acum.T
        a_last = acum[q - 1:q, :]
        dt_e = _split_dot(dt, e_mat, 2)
        ea_e = _split_dot(jnp.exp(acum), e_mat, 2)
        ds_e = _split_dot(jnp.exp(a_last - acum), e_mat, 2)

        bmat = conv(bm_ref, r0, 0, D_SSD, D_SSD + D_BC)
        cmat = conv(cm_ref, r0, 0, D_SSD + D_BC, D_XBC).astype(BF16)
        bt = bmat.T.astype(BF16)

        yn_parts = []
        for g in range(SSD_GROUPS):
            gs = slice(g * gw, (g + 1) * gw)
            xs = conv(x_ref, r0, g * gw, g * gw, (g + 1) * gw)
            xdt = xs * dt_e[:, gs]
            xdt_b = xdt.astype(BF16)
            cg = cmat[:, g * SSD_STATE:(g + 1) * SSD_STATE]
            bgt = bt[g * SSD_STATE:(g + 1) * SSD_STATE, :]
            cb = jnp.dot(cg, bgt, preferred_element_type=F32)
            ht_g = ht_ref[:, gs]
            y_off = jnp.dot(cg, ht_g.astype(BF16), preferred_element_type=F32)
            for e in range(0, e_per_g, heads_per_slab):
                mats = []
                for h in range(g * e_per_g + e, g * e_per_g + e + heads_per_slab):
                    seg = acum[:, h:h + 1] - acum_t[h:h + 1, :]
                    lmat = jnp.exp(jnp.where(causal, seg, -jnp.inf))
                    mats.append((cb * lmat).astype(BF16))
                ls = slice(e * SSD_HEAD_DIM, e * SSD_HEAD_DIM + LANES)
                slab = xdt_b[:, ls]
                diag = jnp.concatenate(
                    [jnp.where(slab_head == i, slab, jnp.zeros_like(slab)) for i in range(heads_per_slab)],
                    axis=0)
                yd_ref[sub, :, g * gw + ls.start:g * gw + ls.stop] = jnp.dot(
                    jnp.concatenate(mats, axis=1), diag, preferred_element_type=F32)
            y = yd_ref[sub, :, gs] + y_off * ea_e[:, gs] + dsk_ref[:, gs] * xs
            xw = (xdt * ds_e[:, gs]).astype(BF16)
            ht_ref[:, gs] = ht_g * ea_e[q - 1:q, gs] + jnp.dot(bgt, xw, preferred_element_type=F32)
            zg = z_ref[0, r0:r0 + q, gs].astype(F32) if full else zin_ref[:, gs]
            yg = y * _silu(zg)
            ms = jnp.mean(yg * yg, axis=-1, keepdims=True)
            yn_parts.append((yg * lax.rsqrt(ms + EPS) * nw_ref[:, gs]).astype(BF16))
        yn = jnp.concatenate(yn_parts, axis=-1)
        br = jnp.dot(yn, wbr_ref[...], preferred_element_type=F32)
        y_ref[0, r0:r0 + t, :] = br[0:t, :].astype(y_ref.dtype)

    for sub in range(nsub):
        chunk(sub)

    @pl.when(c == nc - 1)
    def _():
        last = base if full else base + t
        cso_ref[0] = xin_ref[last - tail:last, :]
        ho_ref[0] = ht_ref[...].T.reshape(SSD_HEADS, SSD_HEAD_DIM, SSD_STATE)


def _ssd(u3, dt3, conv_state, ssm_state, p, out_dtype):
    b, s, _ = u3.shape
    q = SSD_CHUNK
    t = min(q, s)
    nsub = SSD_CHUNKS_PER_STEP if s % (SSD_CHUNKS_PER_STEP * q) == 0 else 1
    rows = nsub * t
    nc = s // rows
    assert s == nc * rows and (t == q or nc == 1)

    def ucol(width, off):
        return pl.BlockSpec((1, rows, width), lambda bi, ci: (bi, ci, off // width))

    tail = CONV_WIDTH - 1
    ti = np.arange(q)
    shift = np.concatenate([ti[None, :] == (ti[:, None] - (tail - j)) for j in range(tail)], axis=0)
    scratch = [
        pltpu.VMEM((2 * SUBLANES if t == q else SUBLANES + q, D_XBC), F32),
        pltpu.VMEM((q, LANES), F32),
        pltpu.VMEM((SSD_STATE, D_SSD), F32),
        pltpu.VMEM((nsub, q, D_SSD), F32),
    ]
    if t < q:
        scratch.append(pltpu.VMEM((q, D_SSD), F32))
    return pl.pallas_call(
        functools.partial(_ssd_kernel, q=q, t=t, nc=nc, nsub=nsub),
        grid=(b, nc),
        in_specs=[
            ucol(D_SSD, U_Z), ucol(D_SSD, U_X), ucol(D_BC, U_B), ucol(D_BC, U_C),
            pl.BlockSpec((1, rows, LANES), lambda bi, ci: (bi, ci, 0)),
            pl.BlockSpec((1, CONV_WIDTH - 1, D_XBC), lambda bi, ci: (bi, 0, 0)),
            pl.BlockSpec((1, SSD_HEADS, SSD_HEAD_DIM, SSD_STATE), lambda bi, ci: (bi, 0, 0, 0)),
            _const_spec((CONV_WIDTH, D_XBC)), _const_spec((1, D_XBC)),
            _const_spec((1, LANES)), _const_spec((1, LANES)),
            _const_spec((1, D_SSD)), _const_spec((1, D_SSD)),
            _const_spec((LANES, D_SSD)), _const_spec((tail * q, q)), _const_spec((D_SSD, D_MODEL)),
        ],
        out_specs=[
            pl.BlockSpec((1, rows, D_MODEL), lambda bi, ci: (bi, ci, 0)),
            pl.BlockSpec((1, CONV_WIDTH - 1, D_XBC), lambda bi, ci: (bi, 0, 0)),
            pl.BlockSpec((1, SSD_HEADS, SSD_HEAD_DIM, SSD_STATE), lambda bi, ci: (bi, 0, 0, 0)),
        ],
        out_shape=[
            jax.ShapeDtypeStruct((b, s, D_MODEL), out_dtype),
            jax.ShapeDtypeStruct((b, CONV_WIDTH - 1, D_XBC), F32),
            jax.ShapeDtypeStruct((b, SSD_HEADS, SSD_HEAD_DIM, SSD_STATE), F32),
        ],
        scratch_shapes=scratch,
        compiler_params=pltpu.CompilerParams(
            dimension_semantics=("parallel", "arbitrary"), vmem_limit_bytes=VMEM_LIMIT),
        name="ssd",
    )(u3, u3, u3, u3, dt3, conv_state, ssm_state, p["conv_w"], p["conv_b"], p["dt_bias"], p["a_neg"],
      p["d_skip"], p["ssd_norm_w"], p["head_expand"], jnp.asarray(shift, BF16), p["w_ssd_br"])


def _block_pieces(win, dil):
    if win <= IN_TM:
        return [(0, win, ATT_BLOCK, ATT_BLOCK)]
    per = IN_TM // dil
    return [(t * IN_TM, 0, per, per) for t in range(win // IN_TM)]


def _attn_prompt_kernel(q_ref, k_ref, v_ref, o_ref, l_ref, kbuf, vtbuf, operm, onat, lperm, lt_ref,
                        s_scr, p_scr, far_scr, inv_scr, *, win, dil):
    n = pl.program_id(1)
    sb, blk, hd = ATT_SB, ATT_BLOCK, ATT_HEAD_DIM
    nsub = sb // win
    nblk = sb // blk
    pieces = _block_pieces(win, dil)
    cur = pl.multiple_of((n % 2) * sb, sb)
    prv = pl.multiple_of(sb - (n % 2) * sb, sb)
    kbuf[pl.ds(cur, sb), :] = k_ref[0]

    @pl.when(n == 0)
    def _():
        kbuf[pl.ds(prv, sb), :] = jnp.zeros((sb, D_ATT), BF16)
        vtbuf[pl.ds(nblk, nblk)] = jnp.zeros((nblk, D_ATT, blk), BF16)

    lt_ref[...] = jnp.zeros((LANES, blk), F32)
    ki = lax.broadcasted_iota(jnp.int32, (blk, blk), 0)
    qi = lax.broadcasted_iota(jnp.int32, (blk, blk), 1)
    upper = ki > qi
    diag = ki == qi
    lane_head = lax.broadcasted_iota(jnp.int32, (blk, LANES), 1) // hd
    nt = (((1,), (1,)), ((), ()))

    def rows(ref, lead, off, sub, r):
        parts = []
        for base, per_sub, per_r, cnt in pieces:
            start = pl.multiple_of(off + base + sub * per_sub + r * per_r, cnt)
            idx = (pl.ds(start, cnt), slice(None))
            parts.append(ref[lead + idx] if lead else ref[idx])
        return parts[0] if len(parts) == 1 else jnp.concatenate(parts, axis=0)

    def block(t, carry):
        sub, r = t // dil, t % dil
        qb = rows(q_ref, (0,), 0, sub, r)
        poff = jnp.where(sub > 0, cur, prv)
        psub = jnp.where(sub > 0, sub - 1, nsub - 1)
        kpc = jnp.concatenate([rows(kbuf, (), poff, psub, r), rows(kbuf, (), cur, sub, r)], axis=0)
        vt_c = rows(v_ref, (0,), 0, sub, r).astype(F32).T.astype(BF16)
        tcur = (n % 2) * nblk + t
        tprv = jnp.where(sub > 0, tcur - dil, (1 - n % 2) * nblk + (nsub - 1) * dil + r)
        vtbuf[tcur] = vt_c
        vt_p = vtbuf[tprv]
        no_prev = jnp.logical_and(n == 0, sub == 0)
        bias = jnp.where(no_prev, -jnp.inf, 0.0).astype(F32)
        zero_q = jnp.zeros((blk, LANES), BF16)
        zero_p = jnp.zeros((blk, blk), BF16)
        for h in range(ATT_SLOTS):
            c, e = divmod(h, LANES // hd)
            cs = slice(c * LANES, (c + 1) * LANES)
            q_h = jnp.where(lane_head == e, qb[:, cs], zero_q)
            st = lax.dot_general(kpc[:, cs], q_h, nt, preferred_element_type=F32)
            st_p = st[0:blk] + bias
            s_scr[h] = jnp.where(upper, st_p, st[blk:2 * blk])
            far_scr[h:h + 1, :] = jnp.sum(jnp.where(diag, st_p, 0.0), axis=0, keepdims=True)
        for h in range(ATT_SLOTS):
            sc = s_scr[h]
            far = far_scr[h:h + 1, :]
            m = jnp.maximum(jnp.max(sc, axis=0, keepdims=True), far)
            p = jnp.exp(sc - m)
            p_far = jnp.exp(far - m)
            inv = 1.0 / (jnp.sum(p, axis=0, keepdims=True) + p_far)
            pb = p.astype(BF16)
            p_scr[h, 0:blk] = jnp.where(upper, pb, zero_p)
            p_scr[h, blk:2 * blk] = jnp.where(upper, zero_p, pb)
            far_scr[h:h + 1, :] = p_far
            inv_scr[h:h + 1, :] = inv
            lt_ref[h:h + 1, :] = m - jnp.log(inv)
        o_parts = []
        for h in range(ATT_SLOTS):
            hs = slice(h * hd, (h + 1) * hd)
            vtp_h = vt_p[hs]
            ot = jnp.dot(jnp.concatenate([vtp_h, vt_c[hs]], axis=1), p_scr[h], preferred_element_type=F32)
            o_parts.append((ot + vtp_h.astype(F32) * far_scr[h:h + 1, :]) * inv_scr[h:h + 1, :])
        start = pl.multiple_of(t * blk, blk)
        operm[pl.ds(start, blk), :] = jnp.concatenate(o_parts, axis=0).T
        lperm[pl.ds(start, blk), :] = lt_ref[...].T
        return carry

    lax.fori_loop(0, sb // blk, block, 0)

    if dil == 1:
        o_ref[0] = operm[...].astype(o_ref.dtype)
        l_ref[0] = lperm[...]
        return
    nslab = D_ATT // LANES
    for sub in range(nsub):
        for r in range(dil):
            src = (sub * dil + r) * blk
            dst = pl.ds(sub * win + r, blk, stride=dil) if dil > 1 else pl.ds(src, blk)
            for c in range(nslab):
                onat[c, dst, :] = operm[src:src + blk, c * LANES:(c + 1) * LANES]
            l_ref[0, dst, :] = lperm[src:src + blk, :]
    for c in range(nslab):
        o_ref[0, :, c * LANES:(c + 1) * LANES] = onat[c].astype(o_ref.dtype)


def _attn_prompt(u2d, b, s, gi):
    win, dil = ATT_GROUPS[gi]
    sb = ATT_SB
    nsb = s // sb
    assert win // dil == ATT_BLOCK and s == nsb * sb and sb % win == 0
    col0 = (U_ATT + gi * U_TILE) // D_ATT

    def ucol(k):
        return pl.BlockSpec((1, sb, D_ATT), lambda bi, n: (bi, n, col0 + k))

    tok = lambda width: pl.BlockSpec((1, sb, width), lambda bi, n: (bi, n, 0))
    u3 = u2d.reshape(b, s, N_U)
    o, lse = pl.pallas_call(
        functools.partial(_attn_prompt_kernel, win=win, dil=dil),
        grid=(b, nsb),
        in_specs=[ucol(0), ucol(1), ucol(2)],
        out_specs=[tok(D_ATT), tok(LANES)],
        out_shape=[
            jax.ShapeDtypeStruct((b, s, D_ATT), BF16),
            jax.ShapeDtypeStruct((b, s, LANES), F32),
        ],
        scratch_shapes=[
            pltpu.VMEM((2 * sb, D_ATT), BF16), pltpu.VMEM((2 * sb // ATT_BLOCK, D_ATT, ATT_BLOCK), BF16),
            pltpu.VMEM((sb, D_ATT), F32), pltpu.VMEM((D_ATT // LANES, sb, LANES), F32),
            pltpu.VMEM((sb, LANES), F32), pltpu.VMEM((LANES, ATT_BLOCK), F32),
            pltpu.VMEM((ATT_SLOTS, ATT_BLOCK, ATT_BLOCK), F32),
            pltpu.VMEM((ATT_SLOTS, 2 * ATT_BLOCK, ATT_BLOCK), BF16),
            pltpu.VMEM((ATT_SLOTS, ATT_BLOCK), F32), pltpu.VMEM((ATT_SLOTS, ATT_BLOCK), F32),
        ],
        compiler_params=pltpu.CompilerParams(
            dimension_semantics=("parallel", "arbitrary"), vmem_limit_bytes=VMEM_LIMIT),
        name=f"attn_prompt_g{gi}",
    )(u3, u3, u3)
    return o.reshape(b * s, D_ATT), lse.reshape(b * s, LANES)


def _kv_tail_kernel(x_ref, o_ref, nat_ref, *, win, dil):
    span = min(win, IN_TM)
    per_slab = LANES // ATT_HEAD_DIM
    for c in range(D_ATT // LANES):
        cs = slice(c * LANES, (c + 1) * LANES)
        if dil == 1:
            nat = x_ref[0, :, cs].astype(F32)
        else:
            for t in range(win // span):
                for dst, src, cnt in _residue_pieces(win, dil, span):
                    nat_ref[c, pl.ds(t * span + src, cnt, stride=dil), :] = (
                        x_ref[0, t * span + dst:t * span + dst + cnt, cs].astype(F32))
            nat = nat_ref[c]
        o_ref[0, 0, c * per_slab:(c + 1) * per_slab] = nat.T.reshape(per_slab, ATT_HEAD_DIM, win)


def _kv_tail(u2d, b, s, gi):
    win, dil = ATT_GROUPS[gi]
    assert s % win == 0 and (win <= IN_TM or win % IN_TM == 0)
    col0 = (U_ATT + gi * U_TILE) // D_ATT + 1
    last = s // win - 1
    u3 = u2d.reshape(b, s, N_U)
    return pl.pallas_call(
        functools.partial(_kv_tail_kernel, win=win, dil=dil),
        grid=(b, 2),
        in_specs=[pl.BlockSpec((1, win, D_ATT), lambda bi, kv: (bi, last, col0 + kv))],
        out_specs=pl.BlockSpec((1, 1, ATT_SLOTS, ATT_HEAD_DIM, win), lambda bi, kv: (bi, kv, 0, 0, 0)),
        out_shape=jax.ShapeDtypeStruct((b, 2, ATT_SLOTS, ATT_HEAD_DIM, win), F32),
        scratch_shapes=[pltpu.VMEM((D_ATT // LANES, win, LANES), F32)],
        compiler_params=pltpu.CompilerParams(
            dimension_semantics=("parallel", "parallel"), vmem_limit_bytes=VMEM_LIMIT),
        name=f"kv_tail_g{gi}",
    )(u3)


def _attn_decode_kernel(x_ref, c0_ref, c1_ref, c2_ref, o_ref, n0_ref, n1_ref, n2_ref):
    caches = (c0_ref, c1_ref, c2_ref)
    news = (n0_ref, n1_ref, n2_ref)
    ng = N_ATT_GROUPS

    for hh in range(x_ref.shape[1]):
        outs_g, lses = [], []
        for g, (win, dil) in enumerate(ATT_GROUPS):
            qc = x_ref[0, hh, g]
            kc = x_ref[0, hh, ng + g]
            vc = x_ref[0, hh, 2 * ng + g]
            kt = caches[g][0, 0, hh]
            vt = caches[g][0, 1, hh]
            lane = lax.broadcasted_iota(jnp.int32, (1, win), 1)
            sc = jnp.sum(kt * qc, axis=0, keepdims=True)
            sc = jnp.where(lane % dil == 0, sc, -jnp.inf)
            sc_new = jnp.sum(kc * qc, axis=0, keepdims=True)
            m = jnp.maximum(jnp.max(sc, axis=1, keepdims=True), sc_new)
            p = jnp.exp(sc - m)
            p_new = jnp.exp(sc_new - m)
            den = jnp.sum(p, axis=1, keepdims=True) + p_new
            outs_g.append((jnp.sum(vt * p, axis=1, keepdims=True) + p_new * vc) / den)
            lses.append(m + jnp.log(den))
            last = lane == win - 1
            news[g][0, 0, hh] = jnp.where(last, kc, pltpu.roll(kt, win - 1, axis=1))
            news[g][0, 1, hh] = jnp.where(last, vc, pltpu.roll(vt, win - 1, axis=1))

        top = jnp.maximum(jnp.maximum(lses[0], lses[1]), lses[2])
        ws = [jnp.exp(l - top) for l in lses]
        o_ref[0, hh] = ((ws[0] * outs_g[0] + ws[1] * outs_g[1] + ws[2] * outs_g[2])
                        / (ws[0] + ws[1] + ws[2]))


def _attn_decode(us, caches):
    b = us.shape[0]
    ng = N_ATT_GROUPS
    hd = ATT_HEAD_DIM
    hb = DEC_HEADS_PER_STEP
    x = us[:, U_ATT:].reshape(b, ng, 3, ATT_SLOTS, hd)
    x = x.transpose(0, 3, 2, 1, 4).reshape(b, ATT_SLOTS, 3 * ng, hd, 1)
    cts = [c.transpose(0, 2, 3, 4, 1) for c in caches]
    cache_specs = []
    for c, (win, dil) in zip(cts, ATT_GROUPS):
        assert c.shape == (b, 2, ATT_SLOTS, hd, win) and win // dil == ATT_BLOCK
        cache_specs.append(pl.BlockSpec((1, 2, hb, hd, win), lambda bi, h: (bi, 0, h, 0, 0)))
    res = pl.pallas_call(
        _attn_decode_kernel,
        grid=(b, ATT_SLOTS // hb),
        in_specs=[pl.BlockSpec((1, hb, 3 * ng, hd, 1), lambda bi, h: (bi, h, 0, 0, 0))] + cache_specs,
        out_specs=[pl.BlockSpec((1, hb, hd, 1), lambda bi, h: (bi, h, 0, 0))] + cache_specs,
        out_shape=[jax.ShapeDtypeStruct((b, ATT_SLOTS, hd, 1), F32)]
        + [jax.ShapeDtypeStruct(c.shape, F32) for c in cts],
        compiler_params=pltpu.CompilerParams(
            dimension_semantics=("parallel", "parallel"), vmem_limit_bytes=VMEM_LIMIT),
        name="attn_decode",
    )(x, *cts)
    o = res[0].reshape(b, D_ATT)
    kvs = [r.transpose(0, 4, 1, 2, 3) for r in res[1:]]
    return o, kvs


def _out_kernel(*refs, n_groups):
    o_refs = refs[:n_groups]
    l_refs = refs[n_groups:2 * n_groups - (n_groups == 1)]
    z_ref, br_ref, gs_ref, ga_ref, x_ref, he_ref, wa_ref, wo_ref, y_ref = refs[len(o_refs) + len(l_refs):]
    if n_groups == 1:
        o = o_refs[0][...].astype(F32)
    else:
        ls = [l[...] for l in l_refs]
        top = functools.reduce(jnp.maximum, ls)
        ws = [jnp.exp(l - top) for l in ls]
        den = functools.reduce(lambda a, c: a + c, ws)
        o = None
        for w, o_ref in zip(ws, o_refs):
            term = _split_dot(w / den, he_ref[...], 2) * o_ref[...].astype(F32)
            o = term if o is None else o + term
    o = o * _silu(z_ref[...].astype(F32))
    br_att = jnp.dot(o.astype(BF16), wa_ref[...], preferred_element_type=F32)
    mix = (_sigmoid(gs_ref[...].astype(F32)) * br_ref[...].astype(F32)
           + _sigmoid(ga_ref[...].astype(F32)) * br_att)
    y_ref[...] = x_ref[...] + jnp.dot(mix.astype(BF16), wo_ref[...], preferred_element_type=F32)


def _out_proj(os_, lses, u2d, br_ssd, x2d, p, tm):
    m = x2d.shape[0]
    row = lambda width, off=0: pl.BlockSpec((tm, width), lambda i: (i, off // width))
    return pl.pallas_call(
        functools.partial(_out_kernel, n_groups=len(os_)),
        grid=(m // tm,),
        in_specs=[row(D_ATT)] * len(os_) + [row(LANES)] * len(lses)
        + [row(D_ATT, U_ZATT), row(D_MODEL), row(D_MODEL, U_GSSD), row(D_MODEL, U_GATT), row(D_MODEL),
           _const_spec((LANES, D_ATT)), _const_spec((D_ATT, D_MODEL)), _const_spec((D_MODEL, D_MODEL))],
        out_specs=row(D_MODEL),
        out_shape=jax.ShapeDtypeStruct((m, D_MODEL), F32),
        compiler_params=pltpu.CompilerParams(
            dimension_semantics=("parallel",), vmem_limit_bytes=VMEM_LIMIT),
        name="out_proj",
    )(*os_, *lses, u2d, br_ssd, u2d, u2d, x2d, p["slot_expand"], p["w_att_br"], p["w_out"])


def _rope_tables(pos):
    half = ATT_HEAD_DIM // 2
    inv = ROPE_THETA ** (-jnp.arange(half, dtype=F32) / half)
    ang = pos.astype(F32)[:, None] * inv[None, :]
    cos, sin = jnp.cos(ang), jnp.sin(ang)
    cos = jnp.concatenate([cos, cos], axis=-1)
    sin = jnp.concatenate([-sin, sin], axis=-1)
    return jnp.tile(cos, (1, LANES // ATT_HEAD_DIM)), jnp.tile(sin, (1, LANES // ATT_HEAD_DIM))


def _storage_positions(s, tm):
    out = []
    for win, dil in ATT_GROUPS:
        idx = np.arange(s)
        if dil > 1:
            for t0 in range(0, s, tm):
                for dst, src, cnt in _residue_pieces(win, dil, tm):
                    idx[t0 + dst:t0 + dst + cnt] = t0 + src + dil * np.arange(cnt)
        out.append(idx)
    return np.stack(out)


def _layer_params(l, norm_w, w_in, conv_w, conv_b, dt_bias, a_log, d_skip, ssd_norm_w, q_norm_w,
                  k_norm_w, w_ssd_br, w_att_br, w_out):
    w = w_in[l]
    o_dt = D_SSD + D_XBC
    o_q = o_dt + SSD_HEADS
    o_k, o_v = o_q + D_ATT_QK, o_q + 2 * D_ATT_QK
    o_zatt = o_q + 3 * D_ATT_QK
    o_gssd, o_gatt = o_zatt + D_ATT, o_zatt + D_ATT + D_MODEL
    cols = [w[:, :o_dt], w[:, o_gssd:o_gatt], w[:, o_gatt:o_gatt + D_MODEL], w[:, o_zatt:o_gssd]]
    for g in range(N_ATT_GROUPS):
        cols += [w[:, o + g * D_ATT:o + (g + 1) * D_ATT] for o in (o_q, o_k, o_v)]
    pad = LANES - SSD_HEADS
    lane_i = np.arange(LANES)
    return dict(
        norm_w=norm_w[l][None, :],
        w_main=jnp.concatenate(cols, axis=1).astype(BF16),
        w_dt=jnp.pad(w[:, o_dt:o_q], ((0, 0), (0, pad))).astype(BF16),
        conv_w=conv_w[l], conv_b=conv_b[l][None, :],
        dt_bias=jnp.pad(dt_bias[l].astype(F32), (0, pad))[None, :],
        a_neg=jnp.pad(-jnp.exp(a_log[l].astype(F32)), (0, pad))[None, :],
        d_skip=jnp.repeat(d_skip[l].astype(F32), SSD_HEAD_DIM)[None, :],
        ssd_norm_w=ssd_norm_w[l][None, :],
        head_expand=jnp.asarray(
            lane_i[:, None] == (np.arange(D_SSD) // SSD_HEAD_DIM)[None, :], BF16),
        slot_expand=jnp.asarray(
            lane_i[:, None] == (np.arange(D_ATT) // ATT_HEAD_DIM)[None, :], BF16),
        head_seg=jnp.asarray(
            lane_i[:, None] // ATT_HEAD_DIM == lane_i[None, :] // ATT_HEAD_DIM, BF16),
        qk_norm_w=jnp.concatenate([jnp.tile(q_norm_w[l].astype(F32) * ATT_SCALE, ATT_SLOTS),
                                   jnp.tile(k_norm_w[l].astype(F32), ATT_SLOTS)])[None, :],
        w_ssd_br=w_ssd_br[l].astype(BF16), w_att_br=w_att_br[l].astype(BF16),
        w_out=w_out[l].astype(BF16),
    )


def _prompt_layer(x, p):
    b, s, _ = x.shape
    assert s % ATT_SB == 0 and ATT_SB % IN_TM == 0
    x2d = x.reshape(b * s, D_MODEL)
    order = jnp.asarray(_storage_positions(s, IN_TM).reshape(-1), jnp.int32)
    cos, sin = _rope_tables(order)
    shape3 = (N_ATT_GROUPS, s, LANES)
    u, dt = _inproj(x2d, p, cos.reshape(shape3), sin.reshape(shape3), BF16, IN_TM, True)
    conv0 = jnp.zeros((b, CONV_WIDTH - 1, D_XBC), F32)
    ssm0 = jnp.zeros((b, SSD_HEADS, SSD_HEAD_DIM, SSD_STATE), F32)
    br_ssd, conv_new, ssm_new = _ssd(u.reshape(b, s, N_U), dt.reshape(b, s, LANES), conv0, ssm0, p, BF16)
    os_, lses, kvs = [], [], []
    for gi in range(N_ATT_GROUPS):
        o, lse = _attn_prompt(u, b, s, gi)
        os_.append(o)
        lses.append(lse)
        kvs.append(_kv_tail(u, b, s, gi).transpose(0, 4, 1, 2, 3))
    y = _out_proj(os_, lses, u, br_ssd.reshape(b * s, D_MODEL), x2d, p, 512)
    return y.reshape(b, s, D_MODEL), conv_new, ssm_new, kvs


def _decode_layer(x, conv_state, ssm_state, caches, p):
    b, s, _ = x.shape
    assert s == 1
    x2d = x.reshape(b, D_MODEL)
    cos, sin = _rope_tables(jnp.full((b,), PAST_LEN, jnp.int32))
    cos3 = jnp.broadcast_to(cos[None], (N_ATT_GROUPS, b, LANES))
    sin3 = jnp.broadcast_to(sin[None], (N_ATT_GROUPS, b, LANES))
    u, dt = _inproj(x2d, p, cos3, sin3, F32, b, False)
    br_ssd, conv_new, ssm_new = _ssd(u.reshape(b, s, N_U), dt.reshape(b, s, LANES),
                                     conv_state, ssm_state, p, F32)
    o, kvs = _attn_decode(u, caches)
    y = _out_proj([o], [], u, br_ssd.reshape(b, D_MODEL), x2d, p, b)
    return y.reshape(b, s, D_MODEL), conv_new, ssm_new, kvs


def kernel(x_prompt, x_sample, state_conv, state_ssm, cache_kv_w128, cache_kv_w512, cache_kv_w2048,
           norm_w, w_in, conv_w, conv_b, dt_bias, a_log, d_skip, ssd_norm_w, q_norm_w, k_norm_w,
           w_ssd_br, w_att_br, w_out):
    yp, ys = x_prompt, x_sample
    outs_p = [[] for _ in range(5)]
    outs_s = [[] for _ in range(5)]
    for l in range(norm_w.shape[0]):
        p = _layer_params(l, norm_w, w_in, conv_w, conv_b, dt_bias, a_log, d_skip, ssd_norm_w,
                          q_norm_w, k_norm_w, w_ssd_br, w_att_br, w_out)
        yp, c, h, kv = _prompt_layer(yp, p)
        for acc, val in zip(outs_p, [c, h] + kv):
            acc.append(val)
        ys, c, h, kv = _decode_layer(ys, state_conv[l], state_ssm[l],
                                     [cache_kv_w128[l], cache_kv_w512[l], cache_kv_w2048[l]], p)
        for acc, val in zip(outs_s, [c, h] + kv):
            acc.append(val)
    return (yp, ys, *[jnp.stack(a) for a in outs_p], *[jnp.stack(a) for a in outs_s])
```

```python
import functools

import numpy as np
import jax
import jax.numpy as jnp
from jax import lax
from jax.experimental import pallas as pl
from jax.experimental.pallas import tpu as pltpu

F32 = jnp.float32
BF16 = jnp.bfloat16

D_MODEL = 1024
D_SSD = 2048
SSD_HEADS = 32
SSD_HEAD_DIM = 64
SSD_GROUPS = 4
SSD_STATE = 128
CONV_WIDTH = 4
D_BC = SSD_GROUPS * SSD_STATE
D_XBC = D_SSD + 2 * D_BC
ATT_HEAD_DIM = 64
ATT_SLOTS = 8
ATT_GROUPS = ((128, 1), (512, 4), (2048, 16))
N_ATT_GROUPS = len(ATT_GROUPS)
D_ATT = ATT_SLOTS * ATT_HEAD_DIM
D_ATT_QK = N_ATT_GROUPS * D_ATT
ATT_SCALE = ATT_HEAD_DIM ** -0.5
ROPE_THETA = 10000.0
EPS = 1e-6
PAST_LEN = 16384

U_Z, U_X, U_B, U_C, U_GSSD, U_GATT, U_ZATT = 0, 2048, 4096, 4608, 5120, 6144, 7168
U_ATT = 7680
U_TILE = 3 * D_ATT
N_U = U_ATT + N_ATT_GROUPS * U_TILE
N_PLAIN_TILES = U_ATT // U_TILE

LANES = 128
SUBLANES = 8
SSD_CHUNK = 128
SSD_CHUNKS_PER_STEP = 4
ATT_BLOCK = 128
ATT_SB = 2048
ATT_STREAMS = 4
IN_TM = 1024
DEC_HEADS_PER_STEP = 4
VMEM_LIMIT = 56 * 1024 * 1024


def _split_dot(a, m, terms, left=False):
    out = None
    r = a
    for t in range(terms):
        p = r.astype(BF16)
        d = (jnp.dot(m, p, preferred_element_type=F32) if left
             else jnp.dot(p, m, preferred_element_type=F32))
        out = d if out is None else out + d
        if t + 1 < terms:
            r = r - p.astype(F32)
    return out


def _sigmoid(x):
    return 1.0 / (1.0 + jnp.exp(-x))


def _silu(x):
    return x * _sigmoid(x)


def _softplus(x):
    return jnp.maximum(x, 0.0) + jnp.log1p(jnp.exp(-jnp.abs(x)))


def _const_spec(shape):
    return pl.BlockSpec(shape, lambda *_: (0,) * len(shape))


def _residue_pieces(win, dil, tm):
    span = min(win, tm)
    per = span // dil
    return [(w * span + r * per, w * span + r, per)
            for w in range(max(tm // win, 1)) for r in range(dil)]


def _inproj_kernel(x_ref, nw_ref, w_ref, wdt_ref, cos_ref, sin_ref, qkw_ref, seg_ref,
                   u_ref, dt_ref, hn_ref, *perm_refs, tm, permute):
    j = pl.program_id(1)

    @pl.when(j == 0)
    def _():
        x = x_ref[...]
        ms = jnp.mean(x * x, axis=-1, keepdims=True)
        hn = x * lax.rsqrt(ms + EPS) * nw_ref[...]
        hn_ref[0] = hn.astype(BF16)
        dt_ref[...] = jnp.dot(hn_ref[0], wdt_ref[...], preferred_element_type=F32)
        if permute:
            (hnf_ref,) = perm_refs
            for c in range(D_MODEL // LANES):
                hnf_ref[c] = hn[:, c * LANES:(c + 1) * LANES]
            for g, (win, dil) in enumerate(ATT_GROUPS):
                if dil == 1:
                    continue
                for dst, src, cnt in _residue_pieces(win, dil, tm):
                    for c in range(D_MODEL // LANES):
                        hn_ref[g, dst:dst + cnt, c * LANES:(c + 1) * LANES] = (
                            hnf_ref[c, pl.ds(src, cnt, stride=dil), :].astype(BF16))

    @pl.when(j < N_PLAIN_TILES)
    def _():
        u_ref[...] = jnp.dot(hn_ref[0], w_ref[...], preferred_element_type=F32).astype(u_ref.dtype)

    half = ATT_HEAD_DIM // 2
    for g, (_, dil) in enumerate(ATT_GROUPS):
        @pl.when(j == N_PLAIN_TILES + g)
        def _(g=g, dil=dil):
            src = g if (permute and dil > 1) else 0
            acc = jnp.dot(hn_ref[src], w_ref[...], preferred_element_type=F32)
            cos, sin, seg = cos_ref[0], sin_ref[0], seg_ref[...]
            lane = lax.broadcasted_iota(jnp.int32, (tm, LANES), 1)
            first = lane % ATT_HEAD_DIM < half
            for c in range(2 * D_ATT // LANES):
                cs = slice(c * LANES, (c + 1) * LANES)
                xs = acc[:, cs]
                ss = _split_dot(xs * xs, seg, 2)
                xn = xs * lax.rsqrt(ss * (1.0 / ATT_HEAD_DIM) + EPS) * qkw_ref[:, cs]
                partner = jnp.where(first, pltpu.roll(xn, LANES - half, axis=1),
                                    pltpu.roll(xn, half, axis=1))
                u_ref[:, cs] = (xn * cos + partner * sin).astype(u_ref.dtype)
            u_ref[:, 2 * D_ATT:] = acc[:, 2 * D_ATT:].astype(u_ref.dtype)


def _inproj(x2d, p, cos3, sin3, out_dtype, tm, permute):
    m = x2d.shape[0]
    period = cos3.shape[1] // tm
    tab = pl.BlockSpec((1, tm, LANES),
                       lambda i, j: (jnp.clip(j - N_PLAIN_TILES, 0, N_ATT_GROUPS - 1), i % period, 0))
    scratch = [pltpu.VMEM((N_ATT_GROUPS if permute else 1, tm, D_MODEL), BF16)]
    if permute:
        scratch.append(pltpu.VMEM((D_MODEL // LANES, tm, LANES), F32))
    return pl.pallas_call(
        functools.partial(_inproj_kernel, tm=tm, permute=permute),
        grid=(m // tm, N_U // U_TILE),
        in_specs=[
            pl.BlockSpec((tm, D_MODEL), lambda i, j: (i, 0)),
            _const_spec((1, D_MODEL)),
            pl.BlockSpec((D_MODEL, U_TILE), lambda i, j: (0, j)),
            _const_spec((D_MODEL, LANES)),
            tab, tab,
            _const_spec((1, 2 * D_ATT)),
            _const_spec((LANES, LANES)),
        ],
        out_specs=[
            pl.BlockSpec((tm, U_TILE), lambda i, j: (i, j)),
            pl.BlockSpec((tm, LANES), lambda i, j: (i, 0)),
        ],
        out_shape=[
            jax.ShapeDtypeStruct((m, N_U), out_dtype),
            jax.ShapeDtypeStruct((m, LANES), F32),
        ],
        scratch_shapes=scratch,
        compiler_params=pltpu.CompilerParams(
            dimension_semantics=("parallel", "arbitrary"), vmem_limit_bytes=VMEM_LIMIT),
        name="inproj",
    )(x2d, p["norm_w"], p["w_main"], p["w_dt"], cos3, sin3, p["qk_norm_w"], p["head_seg"])


def _ssd_kernel(z_ref, x_ref, bm_ref, cm_ref, dt_ref, cs_ref, h0_ref, cw_ref, cb_ref, dtb_ref,
                a_ref, dsk_ref, nw_ref, e_ref, sh_ref, wbr_ref, y_ref, cso_ref, ho_ref,
                xin_ref, dtin_ref, ht_ref, yd_ref, *pad_refs, q, t, nc, nsub):
    c = pl.program_id(1)
    tail = CONV_WIDTH - 1
    base = SUBLANES
    hp = D_SSD
    gw = D_SSD // SSD_GROUPS
    e_per_g = SSD_HEADS // SSD_GROUPS
    full = t == q

    @pl.when(c == 0)
    def _():
        xin_ref[0:2 * base if full else base, :] = jnp.zeros((2 * base if full else base, D_XBC), F32)
        xin_ref[base - tail:base, :] = cs_ref[0]
        ht_ref[...] = h0_ref[0].reshape(hp, SSD_STATE).T

    if not full:
        xin_ref[base:base + q, :] = jnp.zeros((q, D_XBC), F32)
        dtin_ref[...] = jnp.zeros((q, LANES), F32)
        (zin_ref,) = pad_refs
        zin_ref[...] = jnp.zeros((q, D_SSD), F32)
        zin_ref[0:t, :] = z_ref[0].astype(F32)
        xin_ref[base:base + t, 0:D_SSD] = x_ref[0].astype(F32)
        xin_ref[base:base + t, D_SSD:D_SSD + D_BC] = bm_ref[0].astype(F32)
        xin_ref[base:base + t, D_SSD + D_BC:D_XBC] = cm_ref[0].astype(F32)
        dtin_ref[0:t, :] = dt_ref[0]

    def conv(src_ref, r0, off, lo, hi):
        if not full:
            acc = cb_ref[:, lo:hi]
            for j in range(CONV_WIDTH):
                acc = acc + cw_ref[j:j + 1, lo:hi] * xin_ref[base - tail + j:base - tail + j + q, lo:hi]
            return _silu(acc)
        xb = src_ref[0, r0:r0 + q, off:off + hi - lo]
        xf = xb.astype(F32)
        shifted = jnp.dot(sh_ref[...], xb, preferred_element_type=F32)
        acc = cb_ref[:, lo:hi] + cw_ref[tail:tail + 1, lo:hi] * xf
        head = acc[0:base]
        for j in range(tail):
            acc = acc + cw_ref[j:j + 1, lo:hi] * shifted[j * q:(j + 1) * q]
            head = head + cw_ref[j:j + 1, lo:hi] * (
                shifted[j * q:j * q + base] + xin_ref[base - tail + j:2 * base - tail + j, lo:hi])
        xin_ref[0:base, lo:hi] = xf[q - base:q]
        return _silu(jnp.concatenate([head, acc[base:]], axis=0))

    ri = lax.broadcasted_iota(jnp.int32, (q, q), 0)
    ci = lax.broadcasted_iota(jnp.int32, (q, q), 1)
    causal = ri >= ci
    tril = jnp.where(causal, 1.0, 0.0).astype(BF16)
    e_mat = e_ref[...]
    heads_per_slab = LANES // SSD_HEAD_DIM
    slab_head = lax.broadcasted_iota(jnp.int32, (q, LANES), 1) // SSD_HEAD_DIM

    def chunk(sub):
        r0 = sub * q
        dt = _softplus((dt_ref[0, r0:r0 + q, :] if full else dtin_ref[...]) + dtb_ref[...])
        if not full:
            rows = lax.broadcasted_iota(jnp.int32, (q, LANES), 0)
            dt = jnp.where(rows < t, dt, 0.0)
        a = dt * a_ref[...]
        acum = _split_dot(a, tril, 3, left=True)
        acum_t = acum.T
        a_last = acum[q - 1:q, :]
        dt_t = dt.T
        ea_e = _split_dot(jnp.exp(acum), e_mat, 2)
        dtds_e = _split_dot(dt * jnp.exp(a_last - acum), e_mat, 2)

        bmat = conv(bm_ref, r0, 0, D_SSD, D_SSD + D_BC)
        cmat = conv(cm_ref, r0, 0, D_SSD + D_BC, D_XBC).astype(BF16)
        bt = bmat.T.astype(BF16)

        yn_parts = []
        for g in range(SSD_GROUPS):
            gs = slice(g * gw, (g + 1) * gw)
            xs = conv(x_ref, r0, g * gw, g * gw, (g + 1) * gw)
            xs_b = xs.astype(BF16)
            cg = cmat[:, g * SSD_STATE:(g + 1) * SSD_STATE]
            bgt = bt[g * SSD_STATE:(g + 1) * SSD_STATE, :]
            cb = jnp.dot(cg, bgt, preferred_element_type=F32)
            ht_g = ht_ref[:, gs]
            y_off = jnp.dot(cg, ht_g.astype(BF16), preferred_element_type=F32)
            for e in range(0, e_per_g, heads_per_slab):
                mats = []
                for h in range(g * e_per_g + e, g * e_per_g + e + heads_per_slab):
                    seg = acum[:, h:h + 1] - acum_t[h:h + 1, :]
                    lmat = jnp.exp(jnp.where(causal, seg, -jnp.inf))
                    mats.append((cb * lmat * dt_t[h:h + 1, :]).astype(BF16))
                ls = slice(e * SSD_HEAD_DIM, e * SSD_HEAD_DIM + LANES)
                slab = xs_b[:, ls]
                diag = jnp.concatenate(
                    [jnp.where(slab_head == i, slab, jnp.zeros_like(slab)) for i in range(heads_per_slab)],
                    axis=0)
                yd_ref[sub, :, g * gw + ls.start:g * gw + ls.stop] = jnp.dot(
                    jnp.concatenate(mats, axis=1), diag, preferred_element_type=F32)
            y = yd_ref[sub, :, gs] + y_off * ea_e[:, gs] + dsk_ref[:, gs] * xs
            xw = (xs * dtds_e[:, gs]).astype(BF16)
            ht_ref[:, gs] = ht_g * ea_e[q - 1:q, gs] + jnp.dot(bgt, xw, preferred_element_type=F32)
            zg = z_ref[0, r0:r0 + q, gs].astype(F32) if full else zin_ref[:, gs]
            yg = y * _silu(zg)
            ms = jnp.mean(yg * yg, axis=-1, keepdims=True)
            yn_parts.append((yg * lax.rsqrt(ms + EPS) * nw_ref[:, gs]).astype(BF16))
        yn = jnp.concatenate(yn_parts, axis=-1)
        br = jnp.dot(yn, wbr_ref[...], preferred_element_type=F32)
        y_ref[0, r0:r0 + t, :] = br[0:t, :].astype(y_ref.dtype)

    for sub in range(nsub):
        chunk(sub)

    @pl.when(c == nc - 1)
    def _():
        last = base if full else base + t
        cso_ref[0] = xin_ref[last - tail:last, :]
        ho_ref[0] = ht_ref[...].T.reshape(SSD_HEADS, SSD_HEAD_DIM, SSD_STATE)


def _ssd(u3, dt3, conv_state, ssm_state, p, out_dtype):
    b, s, _ = u3.shape
    q = SSD_CHUNK
    t = min(q, s)
    nsub = SSD_CHUNKS_PER_STEP if s % (SSD_CHUNKS_PER_STEP * q) == 0 else 1
    rows = nsub * t
    nc = s // rows
    assert s == nc * rows and (t == q or nc == 1)

    def ucol(width, off):
        return pl.BlockSpec((1, rows, width), lambda bi, ci: (bi, ci, off // width))

    tail = CONV_WIDTH - 1
    ti = np.arange(q)
    shift = np.concatenate([ti[None, :] == (ti[:, None] - (tail - j)) for j in range(tail)], axis=0)
    scratch = [
        pltpu.VMEM((2 * SUBLANES if t == q else SUBLANES + q, D_XBC), F32),
        pltpu.VMEM((q, LANES), F32),
        pltpu.VMEM((SSD_STATE, D_SSD), F32),
        pltpu.VMEM((nsub, q, D_SSD), F32),
    ]
    if t < q:
        scratch.append(pltpu.VMEM((q, D_SSD), F32))
    return pl.pallas_call(
        functools.partial(_ssd_kernel, q=q, t=t, nc=nc, nsub=nsub),
        grid=(b, nc),
        in_specs=[
            ucol(D_SSD, U_Z), ucol(D_SSD, U_X), ucol(D_BC, U_B), ucol(D_BC, U_C),
            pl.BlockSpec((1, rows, LANES), lambda bi, ci: (bi, ci, 0)),
            pl.BlockSpec((1, CONV_WIDTH - 1, D_XBC), lambda bi, ci: (bi, 0, 0)),
            pl.BlockSpec((1, SSD_HEADS, SSD_HEAD_DIM, SSD_STATE), lambda bi, ci: (bi, 0, 0, 0)),
            _const_spec((CONV_WIDTH, D_XBC)), _const_spec((1, D_XBC)),
            _const_spec((1, LANES)), _const_spec((1, LANES)),
            _const_spec((1, D_SSD)), _const_spec((1, D_SSD)),
            _const_spec((LANES, D_SSD)), _const_spec((tail * q, q)), _const_spec((D_SSD, D_MODEL)),
        ],
        out_specs=[
            pl.BlockSpec((1, rows, D_MODEL), lambda bi, ci: (bi, ci, 0)),
            pl.BlockSpec((1, CONV_WIDTH - 1, D_XBC), lambda bi, ci: (bi, 0, 0)),
            pl.BlockSpec((1, SSD_HEADS, SSD_HEAD_DIM, SSD_STATE), lambda bi, ci: (bi, 0, 0, 0)),
        ],
        out_shape=[
            jax.ShapeDtypeStruct((b, s, D_MODEL), out_dtype),
            jax.ShapeDtypeStruct((b, CONV_WIDTH - 1, D_XBC), F32),
            jax.ShapeDtypeStruct((b, SSD_HEADS, SSD_HEAD_DIM, SSD_STATE), F32),
        ],
        scratch_shapes=scratch,
        compiler_params=pltpu.CompilerParams(
            dimension_semantics=("parallel", "arbitrary"), vmem_limit_bytes=VMEM_LIMIT),
        name="ssd",
    )(u3, u3, u3, u3, dt3, conv_state, ssm_state, p["conv_w"], p["conv_b"], p["dt_bias"], p["a_neg"],
      p["d_skip"], p["ssd_norm_w"], p["head_expand"], jnp.asarray(shift, BF16), p["w_ssd_br"])


def _block_pieces(win, dil):
    if win <= IN_TM:
        return [(0, win, ATT_BLOCK, ATT_BLOCK)]
    per = IN_TM // dil
    return [(t * IN_TM, 0, per, per) for t in range(win // IN_TM)]


def _attn_prompt_kernel(q_ref, k_ref, v_ref, o_ref, l_ref, kbuf, vtbuf, operm, onat, lperm, lt_ref,
                        s_scr, p_scr, far_scr, inv_scr, *, win, dil):
    n = pl.program_id(1)
    sb, blk, hd = ATT_SB, ATT_BLOCK, ATT_HEAD_DIM
    nsub = sb // win
    nblk = sb // blk
    pieces = _block_pieces(win, dil)
    cur = pl.multiple_of((n % 2) * sb, sb)
    prv = pl.multiple_of(sb - (n % 2) * sb, sb)
    kbuf[pl.ds(cur, sb), :] = k_ref[0]

    @pl.when(n == 0)
    def _():
        kbuf[pl.ds(prv, sb), :] = jnp.zeros((sb, D_ATT), BF16)
        vtbuf[pl.ds(nblk, nblk)] = jnp.zeros((nblk, D_ATT, blk), BF16)

    lt_ref[...] = jnp.zeros((ATT_STREAMS, LANES, blk), F32)
    ki = lax.broadcasted_iota(jnp.int32, (blk, blk), 0)
    qi = lax.broadcasted_iota(jnp.int32, (blk, blk), 1)
    upper = ki > qi
    diag = ki == qi
    lane_head = lax.broadcasted_iota(jnp.int32, (blk, LANES), 1) // hd
    nt = (((1,), (1,)), ((), ()))

    def rows(ref, lead, off, sub, r):
        parts = []
        for base, per_sub, per_r, cnt in pieces:
            start = pl.multiple_of(off + base + sub * per_sub + r * per_r, cnt)
            idx = (pl.ds(start, cnt), slice(None))
            parts.append(ref[lead + idx] if lead else ref[idx])
        return parts[0] if len(parts) == 1 else jnp.concatenate(parts, axis=0)

    def block(t, w):
        w0 = w * ATT_SLOTS
        sub, r = t // dil, t % dil
        qb = rows(q_ref, (0,), 0, sub, r)
        poff = jnp.where(sub > 0, cur, prv)
        psub = jnp.where(sub > 0, sub - 1, nsub - 1)
        kpc = jnp.concatenate([rows(kbuf, (), poff, psub, r), rows(kbuf, (), cur, sub, r)], axis=0)
        vt_c = rows(v_ref, (0,), 0, sub, r).astype(F32).T.astype(BF16)
        tcur = (n % 2) * nblk + t
        tprv = jnp.where(sub > 0, tcur - dil, (1 - n % 2) * nblk + (nsub - 1) * dil + r)
        vtbuf[tcur] = vt_c
        vt_p = vtbuf[tprv]
        no_prev = jnp.logical_and(n == 0, sub == 0)
        bias = jnp.where(no_prev, -jnp.inf, 0.0).astype(F32)
        zero_q = jnp.zeros((blk, LANES), BF16)
        zero_p = jnp.zeros((blk, blk), BF16)
        for h in range(ATT_SLOTS):
            c, e = divmod(h, LANES // hd)
            cs = slice(c * LANES, (c + 1) * LANES)
            q_h = jnp.where(lane_head == e, qb[:, cs], zero_q)
            st = lax.dot_general(kpc[:, cs], q_h, nt, preferred_element_type=F32)
            st_p = st[0:blk] + bias
            s_scr[w0 + h] = jnp.where(upper, st_p, st[blk:2 * blk])
            far_scr[w0 + h:w0 + h + 1, :] = jnp.sum(
                jnp.where(diag, st_p, 0.0), axis=0, keepdims=True)
        for h in range(ATT_SLOTS):
            sc = s_scr[w0 + h]
            far = far_scr[w0 + h:w0 + h + 1, :]
            m = jnp.maximum(jnp.max(sc, axis=0, keepdims=True), far)
            p = jnp.exp(sc - m)
            p_far = jnp.exp(far - m)
            inv = 1.0 / (jnp.sum(p, axis=0, keepdims=True) + p_far)
            pb = p.astype(BF16)
            p_scr[w0 + h, 0:blk] = jnp.where(upper, pb, zero_p)
            p_scr[w0 + h, blk:2 * blk] = jnp.where(upper, zero_p, pb)
            far_scr[w0 + h:w0 + h + 1, :] = p_far
            inv_scr[w0 + h:w0 + h + 1, :] = inv
            lt_ref[w, h:h + 1, :] = m - jnp.log(inv)
        o_parts = []
        for h in range(ATT_SLOTS):
            hs = slice(h * hd, (h + 1) * hd)
            vtp_h = vt_p[hs]
            ot = jnp.dot(jnp.concatenate([vtp_h, vt_c[hs]], axis=1), p_scr[w0 + h],
                         preferred_element_type=F32)
            o_parts.append((ot + vtp_h.astype(F32) * far_scr[w0 + h:w0 + h + 1, :])
                           * inv_scr[w0 + h:w0 + h + 1, :])
        start = pl.multiple_of(t * blk, blk)
        operm[pl.ds(start, blk), :] = jnp.concatenate(o_parts, axis=0).T
        lperm[pl.ds(start, blk), :] = lt_ref[w].T

    def block_pair(i, carry):
        for w in range(ATT_STREAMS):
            block(i * ATT_STREAMS + w, w)
        return carry

    lax.fori_loop(0, nblk // ATT_STREAMS, block_pair, 0)

    if dil == 1:
        o_ref[0] = operm[...].astype(o_ref.dtype)
        l_ref[0] = lperm[...]
        return
    nslab = D_ATT // LANES
    for sub in range(nsub):
        for r in range(dil):
            src = (sub * dil + r) * blk
            dst = pl.ds(sub * win + r, blk, stride=dil) if dil > 1 else pl.ds(src, blk)
            for c in range(nslab):
                onat[c, dst, :] = operm[src:src + blk, c * LANES:(c + 1) * LANES]
            l_ref[0, dst, :] = lperm[src:src + blk, :]
    for c in range(nslab):
        o_ref[0, :, c * LANES:(c + 1) * LANES] = onat[c].astype(o_ref.dtype)


def _attn_prompt(u2d, b, s, gi):
    win, dil = ATT_GROUPS[gi]
    sb = ATT_SB
    nsb = s // sb
    assert win // dil == ATT_BLOCK and s == nsb * sb and sb % win == 0
    col0 = (U_ATT + gi * U_TILE) // D_ATT

    def ucol(k):
        return pl.BlockSpec((1, sb, D_ATT), lambda bi, n: (bi, n, col0 + k))

    tok = lambda width: pl.BlockSpec((1, sb, width), lambda bi, n: (bi, n, 0))
    u3 = u2d.reshape(b, s, N_U)
    o, lse = pl.pallas_call(
        functools.partial(_attn_prompt_kernel, win=win, dil=dil),
        grid=(b, nsb),
        in_specs=[ucol(0), ucol(1), ucol(2)],
        out_specs=[tok(D_ATT), tok(LANES)],
        out_shape=[
            jax.ShapeDtypeStruct((b, s, D_ATT), BF16),
            jax.ShapeDtypeStruct((b, s, LANES), F32),
        ],
        scratch_shapes=[
            pltpu.VMEM((2 * sb, D_ATT), BF16), pltpu.VMEM((2 * sb // ATT_BLOCK, D_ATT, ATT_BLOCK), BF16),
            pltpu.VMEM((sb, D_ATT), F32), pltpu.VMEM((D_ATT // LANES, sb, LANES), F32),
            pltpu.VMEM((sb, LANES), F32), pltpu.VMEM((ATT_STREAMS, LANES, ATT_BLOCK), F32),
            pltpu.VMEM((ATT_STREAMS * ATT_SLOTS, ATT_BLOCK, ATT_BLOCK), F32),
            pltpu.VMEM((ATT_STREAMS * ATT_SLOTS, 2 * ATT_BLOCK, ATT_BLOCK), BF16),
            pltpu.VMEM((ATT_STREAMS * ATT_SLOTS, ATT_BLOCK), F32),
            pltpu.VMEM((ATT_STREAMS * ATT_SLOTS, ATT_BLOCK), F32),
        ],
        compiler_params=pltpu.CompilerParams(
            dimension_semantics=("parallel", "arbitrary"), vmem_limit_bytes=VMEM_LIMIT),
        name=f"attn_prompt_g{gi}",
    )(u3, u3, u3)
    return o.reshape(b * s, D_ATT), lse.reshape(b * s, LANES)


def _kv_tail_kernel(x_ref, o_ref, nat_ref, *, win, dil):
    span = min(win, IN_TM)
    per_slab = LANES // ATT_HEAD_DIM
    for c in range(D_ATT // LANES):
        cs = slice(c * LANES, (c + 1) * LANES)
        if dil == 1:
            nat = x_ref[0, :, cs].astype(F32)
        else:
            for t in range(win // span):
                for dst, src, cnt in _residue_pieces(win, dil, span):
                    nat_ref[c, pl.ds(t * span + src, cnt, stride=dil), :] = (
                        x_ref[0, t * span + dst:t * span + dst + cnt, cs].astype(F32))
            nat = nat_ref[c]
        o_ref[0, 0, c * per_slab:(c + 1) * per_slab] = nat.T.reshape(per_slab, ATT_HEAD_DIM, win)


def _kv_tail(u2d, b, s, gi):
    win, dil = ATT_GROUPS[gi]
    assert s % win == 0 and (win <= IN_TM or win % IN_TM == 0)
    col0 = (U_ATT + gi * U_TILE) // D_ATT + 1
    last = s // win - 1
    u3 = u2d.reshape(b, s, N_U)
    return pl.pallas_call(
        functools.partial(_kv_tail_kernel, win=win, dil=dil),
        grid=(b, 2),
        in_specs=[pl.BlockSpec((1, win, D_ATT), lambda bi, kv: (bi, last, col0 + kv))],
        out_specs=pl.BlockSpec((1, 1, ATT_SLOTS, ATT_HEAD_DIM, win), lambda bi, kv: (bi, kv, 0, 0, 0)),
        out_shape=jax.ShapeDtypeStruct((b, 2, ATT_SLOTS, ATT_HEAD_DIM, win), F32),
        scratch_shapes=[pltpu.VMEM((D_ATT // LANES, win, LANES), F32)],
        compiler_params=pltpu.CompilerParams(
            dimension_semantics=("parallel", "parallel"), vmem_limit_bytes=VMEM_LIMIT),
        name=f"kv_tail_g{gi}",
    )(u3)


def _attn_decode_kernel(x_ref, c0_ref, c1_ref, c2_ref, o_ref, n0_ref, n1_ref, n2_ref):
    caches = (c0_ref, c1_ref, c2_ref)
    news = (n0_ref, n1_ref, n2_ref)
    ng = N_ATT_GROUPS

    for hh in range(x_ref.shape[1]):
        outs_g, lses = [], []
        for g, (win, dil) in enumerate(ATT_GROUPS):
            qc = x_ref[0, hh, g]
            kc = x_ref[0, hh, ng + g]
            vc = x_ref[0, hh, 2 * ng + g]
            kt = caches[g][0, 0, hh]
            vt = caches[g][0, 1, hh]
            lane = lax.broadcasted_iota(jnp.int32, (1, win), 1)
            sc = jnp.sum(kt * qc, axis=0, keepdims=True)
            sc = jnp.where(lane % dil == 0, sc, -jnp.inf)
            sc_new = jnp.sum(kc * qc, axis=0, keepdims=True)
            m = jnp.maximum(jnp.max(sc, axis=1, keepdims=True), sc_new)
            p = jnp.exp(sc - m)
            p_new = jnp.exp(sc_new - m)
            den = jnp.sum(p, axis=1, keepdims=True) + p_new
            outs_g.append((jnp.sum(vt * p, axis=1, keepdims=True) + p_new * vc) / den)
            lses.append(m + jnp.log(den))
            last = lane == win - 1
            news[g][0, 0, hh] = jnp.where(last, kc, pltpu.roll(kt, win - 1, axis=1))
            news[g][0, 1, hh] = jnp.where(last, vc, pltpu.roll(vt, win - 1, axis=1))

        top = jnp.maximum(jnp.maximum(lses[0], lses[1]), lses[2])
        ws = [jnp.exp(l - top) for l in lses]
        o_ref[0, hh] = ((ws[0] * outs_g[0] + ws[1] * outs_g[1] + ws[2] * outs_g[2])
                        / (ws[0] + ws[1] + ws[2]))


def _attn_decode(us, caches):
    b = us.shape[0]
    ng = N_ATT_GROUPS
    hd = ATT_HEAD_DIM
    hb = DEC_HEADS_PER_STEP
    x = us[:, U_ATT:].reshape(b, ng, 3, ATT_SLOTS, hd)
    x = x.transpose(0, 3, 2, 1, 4).reshape(b, ATT_SLOTS, 3 * ng, hd, 1)
    cts = [c.transpose(0, 2, 3, 4, 1) for c in caches]
    cache_specs = []
    for c, (win, dil) in zip(cts, ATT_GROUPS):
        assert c.shape == (b, 2, ATT_SLOTS, hd, win) and win // dil == ATT_BLOCK
        cache_specs.append(pl.BlockSpec((1, 2, hb, hd, win), lambda bi, h: (bi, 0, h, 0, 0)))
    res = pl.pallas_call(
        _attn_decode_kernel,
        grid=(b, ATT_SLOTS // hb),
        in_specs=[pl.BlockSpec((1, hb, 3 * ng, hd, 1), lambda bi, h: (bi, h, 0, 0, 0))] + cache_specs,
        out_specs=[pl.BlockSpec((1, hb, hd, 1), lambda bi, h: (bi, h, 0, 0))] + cache_specs,
        out_shape=[jax.ShapeDtypeStruct((b, ATT_SLOTS, hd, 1), F32)]
        + [jax.ShapeDtypeStruct(c.shape, F32) for c in cts],
        compiler_params=pltpu.CompilerParams(
            dimension_semantics=("parallel", "parallel"), vmem_limit_bytes=VMEM_LIMIT),
        name="attn_decode",
    )(x, *cts)
    o = res[0].reshape(b, D_ATT)
    kvs = [r.transpose(0, 4, 1, 2, 3) for r in res[1:]]
    return o, kvs


def _out_kernel(*refs, n_groups):
    o_refs = refs[:n_groups]
    l_refs = refs[n_groups:2 * n_groups - (n_groups == 1)]
    z_ref, br_ref, gs_ref, ga_ref, x_ref, he_ref, wa_ref, wo_ref, y_ref = refs[len(o_refs) + len(l_refs):]
    if n_groups == 1:
        o = o_refs[0][...].astype(F32)
    else:
        ls = [l[...] for l in l_refs]
        top = functools.reduce(jnp.maximum, ls)
        ws = [jnp.exp(l - top) for l in ls]
        den = functools.reduce(lambda a, c: a + c, ws)
        o = None
        for w, o_ref in zip(ws, o_refs):
            term = _split_dot(w / den, he_ref[...], 2) * o_ref[...].astype(F32)
            o = term if o is None else o + term
    o = o * _silu(z_ref[...].astype(F32))
    br_att = jnp.dot(o.astype(BF16), wa_ref[...], preferred_element_type=F32)
    mix = (_sigmoid(gs_ref[...].astype(F32)) * br_ref[...].astype(F32)
           + _sigmoid(ga_ref[...].astype(F32)) * br_att)
    y_ref[...] = x_ref[...] + jnp.dot(mix.astype(BF16), wo_ref[...], preferred_element_type=F32)


def _out_proj(os_, lses, u2d, br_ssd, x2d, p, tm):
    m = x2d.shape[0]
    row = lambda width, off=0: pl.BlockSpec((tm, width), lambda i: (i, off // width))
    return pl.pallas_call(
        functools.partial(_out_kernel, n_groups=len(os_)),
        grid=(m // tm,),
        in_specs=[row(D_ATT)] * len(os_) + [row(LANES)] * len(lses)
        + [row(D_ATT, U_ZATT), row(D_MODEL), row(D_MODEL, U_GSSD), row(D_MODEL, U_GATT), row(D_MODEL),
           _const_spec((LANES, D_ATT)), _const_spec((D_ATT, D_MODEL)), _const_spec((D_MODEL, D_MODEL))],
        out_specs=row(D_MODEL),
        out_shape=jax.ShapeDtypeStruct((m, D_MODEL), F32),
        compiler_params=pltpu.CompilerParams(
            dimension_semantics=("parallel",), vmem_limit_bytes=VMEM_LIMIT),
        name="out_proj",
    )(*os_, *lses, u2d, br_ssd, u2d, u2d, x2d, p["slot_expand"], p["w_att_br"], p["w_out"])


def _rope_tables(pos):
    half = ATT_HEAD_DIM // 2
    inv = ROPE_THETA ** (-jnp.arange(half, dtype=F32) / half)
    ang = pos.astype(F32)[:, None] * inv[None, :]
    cos, sin = jnp.cos(ang), jnp.sin(ang)
    cos = jnp.concatenate([cos, cos], axis=-1)
    sin = jnp.concatenate([-sin, sin], axis=-1)
    return jnp.tile(cos, (1, LANES // ATT_HEAD_DIM)), jnp.tile(sin, (1, LANES // ATT_HEAD_DIM))


def _storage_positions(s, tm):
    out = []
    for win, dil in ATT_GROUPS:
        idx = np.arange(s)
        if dil > 1:
            for t0 in range(0, s, tm):
                for dst, src, cnt in _residue_pieces(win, dil, tm):
                    idx[t0 + dst:t0 + dst + cnt] = t0 + src + dil * np.arange(cnt)
        out.append(idx)
    return np.stack(out)


def _layer_params(l, norm_w, w_in, conv_w, conv_b, dt_bias, a_log, d_skip, ssd_norm_w, q_norm_w,
                  k_norm_w, w_ssd_br, w_att_br, w_out):
    w = w_in[l]
    o_dt = D_SSD + D_XBC
    o_q = o_dt + SSD_HEADS
    o_k, o_v = o_q + D_ATT_QK, o_q + 2 * D_ATT_QK
    o_zatt = o_q + 3 * D_ATT_QK
    o_gssd, o_gatt = o_zatt + D_ATT, o_zatt + D_ATT + D_MODEL
    cols = [w[:, :o_dt], w[:, o_gssd:o_gatt], w[:, o_gatt:o_gatt + D_MODEL], w[:, o_zatt:o_gssd]]
    for g in range(N_ATT_GROUPS):
        cols += [w[:, o + g * D_ATT:o + (g + 1) * D_ATT] for o in (o_q, o_k, o_v)]
    pad = LANES - SSD_HEADS
    lane_i = np.arange(LANES)
    return dict(
        norm_w=norm_w[l][None, :],
        w_main=jnp.concatenate(cols, axis=1).astype(BF16),
        w_dt=jnp.pad(w[:, o_dt:o_q], ((0, 0), (0, pad))).astype(BF16),
        conv_w=conv_w[l], conv_b=conv_b[l][None, :],
        dt_bias=jnp.pad(dt_bias[l].astype(F32), (0, pad))[None, :],
        a_neg=jnp.pad(-jnp.exp(a_log[l].astype(F32)), (0, pad))[None, :],
        d_skip=jnp.repeat(d_skip[l].astype(F32), SSD_HEAD_DIM)[None, :],
        ssd_norm_w=ssd_norm_w[l][None, :],
        head_expand=jnp.asarray(
            lane_i[:, None] == (np.arange(D_SSD) // SSD_HEAD_DIM)[None, :], BF16),
        slot_expand=jnp.asarray(
            lane_i[:, None] == (np.arange(D_ATT) // ATT_HEAD_DIM)[None, :], BF16),
        head_seg=jnp.asarray(
            lane_i[:, None] // ATT_HEAD_DIM == lane_i[None, :] // ATT_HEAD_DIM, BF16),
        qk_norm_w=jnp.concatenate([jnp.tile(q_norm_w[l].astype(F32) * ATT_SCALE, ATT_SLOTS),
                                   jnp.tile(k_norm_w[l].astype(F32), ATT_SLOTS)])[None, :],
        w_ssd_br=w_ssd_br[l].astype(BF16), w_att_br=w_att_br[l].astype(BF16),
        w_out=w_out[l].astype(BF16),
    )


def _prompt_layer(x, p):
    b, s, _ = x.shape
    assert s % ATT_SB == 0 and ATT_SB % IN_TM == 0
    x2d = x.reshape(b * s, D_MODEL)
    order = jnp.asarray(_storage_positions(s, IN_TM).reshape(-1), jnp.int32)
    cos, sin = _rope_tables(order)
    shape3 = (N_ATT_GROUPS, s, LANES)
    u, dt = _inproj(x2d, p, cos.reshape(shape3), sin.reshape(shape3), BF16, IN_TM, True)
    conv0 = jnp.zeros((b, CONV_WIDTH - 1, D_XBC), F32)
    ssm0 = jnp.zeros((b, SSD_HEADS, SSD_HEAD_DIM, SSD_STATE), F32)
    br_ssd, conv_new, ssm_new = _ssd(u.reshape(b, s, N_U), dt.reshape(b, s, LANES), conv0, ssm0, p, BF16)
    os_, lses, kvs = [], [], []
    for gi in range(N_ATT_GROUPS):
        o, lse = _attn_prompt(u, b, s, gi)
        os_.append(o)
        lses.append(lse)
        kvs.append(_kv_tail(u, b, s, gi).transpose(0, 4, 1, 2, 3))
    y = _out_proj(os_, lses, u, br_ssd.reshape(b * s, D_MODEL), x2d, p, 512)
    return y.reshape(b, s, D_MODEL), conv_new, ssm_new, kvs


def _decode_layer(x, conv_state, ssm_state, caches, p):
    b, s, _ = x.shape
    assert s == 1
    x2d = x.reshape(b, D_MODEL)
    cos, sin = _rope_tables(jnp.full((b,), PAST_LEN, jnp.int32))
    cos3 = jnp.broadcast_to(cos[None], (N_ATT_GROUPS, b, LANES))
    sin3 = jnp.broadcast_to(sin[None], (N_ATT_GROUPS, b, LANES))
    u, dt = _inproj(x2d, p, cos3, sin3, F32, b, False)
    br_ssd, conv_new, ssm_new = _ssd(u.reshape(b, s, N_U), dt.reshape(b, s, LANES),
                                     conv_state, ssm_state, p, F32)
    o, kvs = _attn_decode(u, caches)
    y = _out_proj([o], [], u, br_ssd.reshape(b, D_MODEL), x2d, p, b)
    return y.reshape(b, s, D_MODEL), conv_new, ssm_new, kvs


def kernel(x_prompt, x_sample, state_conv, state_ssm, cache_kv_w128, cache_kv_w512, cache_kv_w2048,
           norm_w, w_in, conv_w, conv_b, dt_bias, a_log, d_skip, ssd_norm_w, q_norm_w, k_norm_w,
           w_ssd_br, w_att_br, w_out):
    yp, ys = x_prompt, x_sample
    outs_p = [[] for _ in range(5)]
    outs_s = [[] for _ in range(5)]
    for l in range(norm_w.shape[0]):
        p = _layer_params(l, norm_w, w_in, conv_w, conv_b, dt_bias, a_log, d_skip, ssd_norm_w,
                          q_norm_w, k_norm_w, w_ssd_br, w_att_br, w_out)
        yp, c, h, kv = _prompt_layer(yp, p)
        for acc, val in zip(outs_p, [c, h] + kv):
            acc.append(val)
        ys, c, h, kv = _decode_layer(ys, state_conv[l], state_ssm[l],
                                     [cache_kv_w128[l], cache_kv_w512[l], cache_kv_w2048[l]], p)
        for acc, val in zip(outs_s, [c, h] + kv):
            acc.append(val)
    return (yp, ys, *[jnp.stack(a) for a in outs_p], *[jnp.stack(a) for a in outs_s])
```

```python
import functools

import numpy as np
import jax
import jax.numpy as jnp
from jax import lax
from jax.experimental import pallas as pl
from jax.experimental.pallas import tpu as pltpu

F32 = jnp.float32
BF16 = jnp.bfloat16

D_MODEL = 1024
D_SSD = 2048
SSD_HEADS = 32
SSD_HEAD_DIM = 64
SSD_GROUPS = 4
SSD_STATE = 128
CONV_WIDTH = 4
D_BC = SSD_GROUPS * SSD_STATE
D_XBC = D_SSD + 2 * D_BC
ATT_HEAD_DIM = 64
ATT_SLOTS = 8
ATT_GROUPS = ((128, 1), (512, 4), (2048, 16))
N_ATT_GROUPS = len(ATT_GROUPS)
D_ATT = ATT_SLOTS * ATT_HEAD_DIM
D_ATT_QK = N_ATT_GROUPS * D_ATT
ATT_SCALE = ATT_HEAD_DIM ** -0.5
ROPE_THETA = 10000.0
EPS = 1e-6
PAST_LEN = 16384

U_Z, U_X, U_B, U_C, U_GSSD, U_GATT, U_ZATT = 0, 2048, 4096, 4608, 5120, 6144, 7168
U_ATT = 7680
U_TILE = 3 * D_ATT
N_U = U_ATT + N_ATT_GROUPS * U_TILE
N_PLAIN_TILES = U_ATT // U_TILE

LANES = 128
SUBLANES = 8
SSD_CHUNK = 128
SSD_CHUNKS_PER_STEP = 4
ATT_BLOCK = 128
ATT_SB = 2048
ATT_STREAMS = 8
IN_TM = 1024
DEC_HEADS_PER_STEP = 4
VMEM_LIMIT = 56 * 1024 * 1024


def _split_dot(a, m, terms, left=False):
    out = None
    r = a
    for t in range(terms):
        p = r.astype(BF16)
        d = (jnp.dot(m, p, preferred_element_type=F32) if left
             else jnp.dot(p, m, preferred_element_type=F32))
        out = d if out is None else out + d
        if t + 1 < terms:
            r = r - p.astype(F32)
    return out


def _sigmoid(x):
    return 1.0 / (1.0 + jnp.exp(-x))


def _silu(x):
    return x * _sigmoid(x)


def _softplus(x):
    return jnp.maximum(x, 0.0) + jnp.log1p(jnp.exp(-jnp.abs(x)))


def _const_spec(shape):
    return pl.BlockSpec(shape, lambda *_: (0,) * len(shape))


def _residue_pieces(win, dil, tm):
    span = min(win, tm)
    per = span // dil
    return [(w * span + r * per, w * span + r, per)
            for w in range(max(tm // win, 1)) for r in range(dil)]


def _inproj_kernel(x_ref, nw_ref, w_ref, wdt_ref, cos_ref, sin_ref, qkw_ref, seg_ref,
                   u_ref, dt_ref, hn_ref, *perm_refs, tm, permute):
    j = pl.program_id(1)

    @pl.when(j == 0)
    def _():
        x = x_ref[...]
        ms = jnp.mean(x * x, axis=-1, keepdims=True)
        hn = x * lax.rsqrt(ms + EPS) * nw_ref[...]
        hn_ref[0] = hn.astype(BF16)
        dt_ref[...] = jnp.dot(hn_ref[0], wdt_ref[...], preferred_element_type=F32)
        if permute:
            (hnf_ref,) = perm_refs
            for c in range(D_MODEL // LANES):
                hnf_ref[c] = hn[:, c * LANES:(c + 1) * LANES]
            for g, (win, dil) in enumerate(ATT_GROUPS):
                if dil == 1:
                    continue
                for dst, src, cnt in _residue_pieces(win, dil, tm):
                    for c in range(D_MODEL // LANES):
                        hn_ref[g, dst:dst + cnt, c * LANES:(c + 1) * LANES] = (
                            hnf_ref[c, pl.ds(src, cnt, stride=dil), :].astype(BF16))

    @pl.when(j < N_PLAIN_TILES)
    def _():
        u_ref[...] = jnp.dot(hn_ref[0], w_ref[...], preferred_element_type=F32).astype(u_ref.dtype)

    half = ATT_HEAD_DIM // 2
    for g, (_, dil) in enumerate(ATT_GROUPS):
        @pl.when(j == N_PLAIN_TILES + g)
        def _(g=g, dil=dil):
            src = g if (permute and dil > 1) else 0
            acc = jnp.dot(hn_ref[src], w_ref[...], preferred_element_type=F32)
            cos, sin, seg = cos_ref[0], sin_ref[0], seg_ref[...]
            lane = lax.broadcasted_iota(jnp.int32, (tm, LANES), 1)
            first = lane % ATT_HEAD_DIM < half
            for c in range(2 * D_ATT // LANES):
                cs = slice(c * LANES, (c + 1) * LANES)
                xs = acc[:, cs]
                ss = _split_dot(xs * xs, seg, 2)
                xn = xs * lax.rsqrt(ss * (1.0 / ATT_HEAD_DIM) + EPS) * qkw_ref[:, cs]
                partner = jnp.where(first, pltpu.roll(xn, LANES - half, axis=1),
                                    pltpu.roll(xn, half, axis=1))
                u_ref[:, cs] = (xn * cos + partner * sin).astype(u_ref.dtype)
            u_ref[:, 2 * D_ATT:] = acc[:, 2 * D_ATT:].astype(u_ref.dtype)


def _inproj(x2d, p, cos3, sin3, out_dtype, tm, permute):
    m = x2d.shape[0]
    period = cos3.shape[1] // tm
    tab = pl.BlockSpec((1, tm, LANES),
                       lambda i, j: (jnp.clip(j - N_PLAIN_TILES, 0, N_ATT_GROUPS - 1), i % period, 0))
    scratch = [pltpu.VMEM((N_ATT_GROUPS if permute else 1, tm, D_MODEL), BF16)]
    if permute:
        scratch.append(pltpu.VMEM((D_MODEL // LANES, tm, LANES), F32))
    return pl.pallas_call(
        functools.partial(_inproj_kernel, tm=tm, permute=permute),
        grid=(m // tm, N_U // U_TILE),
        in_specs=[
            pl.BlockSpec((tm, D_MODEL), lambda i, j: (i, 0)),
            _const_spec((1, D_MODEL)),
            pl.BlockSpec((D_MODEL, U_TILE), lambda i, j: (0, j)),
            _const_spec((D_MODEL, LANES)),
            tab, tab,
            _const_spec((1, 2 * D_ATT)),
            _const_spec((LANES, LANES)),
        ],
        out_specs=[
            pl.BlockSpec((tm, U_TILE), lambda i, j: (i, j)),
            pl.BlockSpec((tm, LANES), lambda i, j: (i, 0)),
        ],
        out_shape=[
            jax.ShapeDtypeStruct((m, N_U), out_dtype),
            jax.ShapeDtypeStruct((m, LANES), F32),
        ],
        scratch_shapes=scratch,
        compiler_params=pltpu.CompilerParams(
            dimension_semantics=("parallel", "arbitrary"), vmem_limit_bytes=VMEM_LIMIT),
        name="inproj",
    )(x2d, p["norm_w"], p["w_main"], p["w_dt"], cos3, sin3, p["qk_norm_w"], p["head_seg"])


def _ssd_kernel(z_ref, x_ref, bm_ref, cm_ref, dt_ref, cs_ref, h0_ref, cw_ref, cb_ref, dtb_ref,
                a_ref, dsk_ref, nw_ref, e_ref, sh_ref, wbr_ref, y_ref, cso_ref, ho_ref,
                xin_ref, dtin_ref, ht_ref, yd_ref, *pad_refs, q, t, nc, nsub):
    c = pl.program_id(1)
    tail = CONV_WIDTH - 1
    base = SUBLANES
    hp = D_SSD
    gw = D_SSD // SSD_GROUPS
    e_per_g = SSD_HEADS // SSD_GROUPS
    full = t == q

    @pl.when(c == 0)
    def _():
        xin_ref[0:2 * base if full else base, :] = jnp.zeros((2 * base if full else base, D_XBC), F32)
        xin_ref[base - tail:base, :] = cs_ref[0]
        ht_ref[...] = h0_ref[0].reshape(hp, SSD_STATE).T

    if not full:
        xin_ref[base:base + q, :] = jnp.zeros((q, D_XBC), F32)
        dtin_ref[...] = jnp.zeros((q, LANES), F32)
        (zin_ref,) = pad_refs
        zin_ref[...] = jnp.zeros((q, D_SSD), F32)
        zin_ref[0:t, :] = z_ref[0].astype(F32)
        xin_ref[base:base + t, 0:D_SSD] = x_ref[0].astype(F32)
        xin_ref[base:base + t, D_SSD:D_SSD + D_BC] = bm_ref[0].astype(F32)
        xin_ref[base:base + t, D_SSD + D_BC:D_XBC] = cm_ref[0].astype(F32)
        dtin_ref[0:t, :] = dt_ref[0]

    def conv(src_ref, r0, off, lo, hi):
        if not full:
            acc = cb_ref[:, lo:hi]
            for j in range(CONV_WIDTH):
                acc = acc + cw_ref[j:j + 1, lo:hi] * xin_ref[base - tail + j:base - tail + j + q, lo:hi]
            return _silu(acc)
        xb = src_ref[0, r0:r0 + q, off:off + hi - lo]
        xf = xb.astype(F32)
        shifted = jnp.dot(sh_ref[...], xb, preferred_element_type=F32)
        acc = cb_ref[:, lo:hi] + cw_ref[tail:tail + 1, lo:hi] * xf
        head = acc[0:base]
        for j in range(tail):
            acc = acc + cw_ref[j:j + 1, lo:hi] * shifted[j * q:(j + 1) * q]
            head = head + cw_ref[j:j + 1, lo:hi] * (
                shifted[j * q:j * q + base] + xin_ref[base - tail + j:2 * base - tail + j, lo:hi])
        xin_ref[0:base, lo:hi] = xf[q - base:q]
        return _silu(jnp.concatenate([head, acc[base:]], axis=0))

    ri = lax.broadcasted_iota(jnp.int32, (q, q), 0)
    ci = lax.broadcasted_iota(jnp.int32, (q, q), 1)
    causal = ri >= ci
    tril = jnp.where(causal, 1.0, 0.0).astype(BF16)
    e_mat = e_ref[...]
    heads_per_slab = LANES // SSD_HEAD_DIM
    slab_head = lax.broadcasted_iota(jnp.int32, (q, LANES), 1) // SSD_HEAD_DIM

    def chunk(sub):
        r0 = sub * q
        dt = _softplus((dt_ref[0, r0:r0 + q, :] if full else dtin_ref[...]) + dtb_ref[...])
        if not full:
            rows = lax.broadcasted_iota(jnp.int32, (q, LANES), 0)
            dt = jnp.where(rows < t, dt, 0.0)
        a = dt * a_ref[...]
        acum = _split_dot(a, tril, 3, left=True)
        acum_t = acum.T
        a_last = acum[q - 1:q, :]
        dt_t = dt.T
        ea_e = _split_dot(jnp.exp(acum), e_mat, 2)
        dtds_e = _split_dot(dt * jnp.exp(a_last - acum), e_mat, 2)

        bmat = conv(bm_ref, r0, 0, D_SSD, D_SSD + D_BC)
        cmat = conv(cm_ref, r0, 0, D_SSD + D_BC, D_XBC).astype(BF16)
        bt = bmat.T.astype(BF16)

        yn_parts = []
        for g in range(SSD_GROUPS):
            gs = slice(g * gw, (g + 1) * gw)
            xs = conv(x_ref, r0, g * gw, g * gw, (g + 1) * gw)
            xs_b = xs.astype(BF16)
            cg = cmat[:, g * SSD_STATE:(g + 1) * SSD_STATE]
            bgt = bt[g * SSD_STATE:(g + 1) * SSD_STATE, :]
            cb = jnp.dot(cg, bgt, preferred_element_type=F32)
            ht_g = ht_ref[:, gs]
            y_off = jnp.dot(cg, ht_g.astype(BF16), preferred_element_type=F32)
            for e in range(0, e_per_g, heads_per_slab):
                mats = []
                for h in range(g * e_per_g + e, g * e_per_g + e + heads_per_slab):
                    seg = acum[:, h:h + 1] - acum_t[h:h + 1, :]
                    lmat = jnp.exp(jnp.where(causal, seg, -jnp.inf))
                    mats.append((cb * lmat * dt_t[h:h + 1, :]).astype(BF16))
                ls = slice(e * SSD_HEAD_DIM, e * SSD_HEAD_DIM + LANES)
                slab = xs_b[:, ls]
                diag = jnp.concatenate(
                    [jnp.where(slab_head == i, slab, jnp.zeros_like(slab)) for i in range(heads_per_slab)],
                    axis=0)
                yd_ref[sub, :, g * gw + ls.start:g * gw + ls.stop] = jnp.dot(
                    jnp.concatenate(mats, axis=1), diag, preferred_element_type=F32)
            y = yd_ref[sub, :, gs] + y_off * ea_e[:, gs] + dsk_ref[:, gs] * xs
            xw = (xs * dtds_e[:, gs]).astype(BF16)
            ht_ref[:, gs] = ht_g * ea_e[q - 1:q, gs] + jnp.dot(bgt, xw, preferred_element_type=F32)
            zg = z_ref[0, r0:r0 + q, gs].astype(F32) if full else zin_ref[:, gs]
            yg = y * _silu(zg)
            ms = jnp.mean(yg * yg, axis=-1, keepdims=True)
            yn_parts.append((yg * lax.rsqrt(ms + EPS) * nw_ref[:, gs]).astype(BF16))
        yn = jnp.concatenate(yn_parts, axis=-1)
        br = jnp.dot(yn, wbr_ref[...], preferred_element_type=F32)
        y_ref[0, r0:r0 + t, :] = br[0:t, :].astype(y_ref.dtype)

    for sub in range(nsub):
        chunk(sub)

    @pl.when(c == nc - 1)
    def _():
        last = base if full else base + t
        cso_ref[0] = xin_ref[last - tail:last, :]
        ho_ref[0] = ht_ref[...].T.reshape(SSD_HEADS, SSD_HEAD_DIM, SSD_STATE)


def _ssd(u3, dt3, conv_state, ssm_state, p, out_dtype):
    b, s, _ = u3.shape
    q = SSD_CHUNK
    t = min(q, s)
    nsub = SSD_CHUNKS_PER_STEP if s % (SSD_CHUNKS_PER_STEP * q) == 0 else 1
    rows = nsub * t
    nc = s // rows
    assert s == nc * rows and (t == q or nc == 1)

    def ucol(width, off):
        return pl.BlockSpec((1, rows, width), lambda bi, ci: (bi, ci, off // width))

    tail = CONV_WIDTH - 1
    ti = np.arange(q)
    shift = np.concatenate([ti[None, :] == (ti[:, None] - (tail - j)) for j in range(tail)], axis=0)
    scratch = [
        pltpu.VMEM((2 * SUBLANES if t == q else SUBLANES + q, D_XBC), F32),
        pltpu.VMEM((q, LANES), F32),
        pltpu.VMEM((SSD_STATE, D_SSD), F32),
        pltpu.VMEM((nsub, q, D_SSD), F32),
    ]
    if t < q:
        scratch.append(pltpu.VMEM((q, D_SSD), F32))
    return pl.pallas_call(
        functools.partial(_ssd_kernel, q=q, t=t, nc=nc, nsub=nsub),
        grid=(b, nc),
        in_specs=[
            ucol(D_SSD, U_Z), ucol(D_SSD, U_X), ucol(D_BC, U_B), ucol(D_BC, U_C),
            pl.BlockSpec((1, rows, LANES), lambda bi, ci: (bi, ci, 0)),
            pl.BlockSpec((1, CONV_WIDTH - 1, D_XBC), lambda bi, ci: (bi, 0, 0)),
            pl.BlockSpec((1, SSD_HEADS, SSD_HEAD_DIM, SSD_STATE), lambda bi, ci: (bi, 0, 0, 0)),
            _const_spec((CONV_WIDTH, D_XBC)), _const_spec((1, D_XBC)),
            _const_spec((1, LANES)), _const_spec((1, LANES)),
            _const_spec((1, D_SSD)), _const_spec((1, D_SSD)),
            _const_spec((LANES, D_SSD)), _const_spec((tail * q, q)), _const_spec((D_SSD, D_MODEL)),
        ],
        out_specs=[
            pl.BlockSpec((1, rows, D_MODEL), lambda bi, ci: (bi, ci, 0)),
            pl.BlockSpec((1, CONV_WIDTH - 1, D_XBC), lambda bi, ci: (bi, 0, 0)),
            pl.BlockSpec((1, SSD_HEADS, SSD_HEAD_DIM, SSD_STATE), lambda bi, ci: (bi, 0, 0, 0)),
        ],
        out_shape=[
            jax.ShapeDtypeStruct((b, s, D_MODEL), out_dtype),
            jax.ShapeDtypeStruct((b, CONV_WIDTH - 1, D_XBC), F32),
            jax.ShapeDtypeStruct((b, SSD_HEADS, SSD_HEAD_DIM, SSD_STATE), F32),
        ],
        scratch_shapes=scratch,
        compiler_params=pltpu.CompilerParams(
            dimension_semantics=("parallel", "arbitrary"), vmem_limit_bytes=VMEM_LIMIT),
        name="ssd",
    )(u3, u3, u3, u3, dt3, conv_state, ssm_state, p["conv_w"], p["conv_b"], p["dt_bias"], p["a_neg"],
      p["d_skip"], p["ssd_norm_w"], p["head_expand"], jnp.asarray(shift, BF16), p["w_ssd_br"])


def _block_pieces(win, dil):
    if win <= IN_TM:
        return [(0, win, ATT_BLOCK, ATT_BLOCK)]
    per = IN_TM // dil
    return [(t * IN_TM, 0, per, per) for t in range(win // IN_TM)]


def _attn_prompt_kernel(q_ref, k_ref, v_ref, o_ref, l_ref, kbuf, vtbuf, operm, onat, lperm, lt_ref,
                        s_scr, p_scr, far_scr, inv_scr, *, win, dil):
    n = pl.program_id(1)
    sb, blk, hd = ATT_SB, ATT_BLOCK, ATT_HEAD_DIM
    nsub = sb // win
    nblk = sb // blk
    pieces = _block_pieces(win, dil)
    cur = pl.multiple_of((n % 2) * sb, sb)
    prv = pl.multiple_of(sb - (n % 2) * sb, sb)
    kbuf[pl.ds(cur, sb), :] = k_ref[0]

    @pl.when(n == 0)
    def _():
        kbuf[pl.ds(prv, sb), :] = jnp.zeros((sb, D_ATT), BF16)
        vtbuf[pl.ds(nblk, nblk)] = jnp.zeros((nblk, D_ATT, blk), BF16)

    lt_ref[...] = jnp.zeros((ATT_STREAMS, LANES, blk), F32)
    ki = lax.broadcasted_iota(jnp.int32, (blk, blk), 0)
    qi = lax.broadcasted_iota(jnp.int32, (blk, blk), 1)
    upper = ki > qi
    diag = ki == qi
    lane_head = lax.broadcasted_iota(jnp.int32, (blk, LANES), 1) // hd
    nt = (((1,), (1,)), ((), ()))

    def rows(ref, lead, off, sub, r):
        parts = []
        for base, per_sub, per_r, cnt in pieces:
            start = pl.multiple_of(off + base + sub * per_sub + r * per_r, cnt)
            idx = (pl.ds(start, cnt), slice(None))
            parts.append(ref[lead + idx] if lead else ref[idx])
        return parts[0] if len(parts) == 1 else jnp.concatenate(parts, axis=0)

    def block(t, w):
        w0 = w * ATT_SLOTS
        sub, r = t // dil, t % dil
        qb = rows(q_ref, (0,), 0, sub, r)
        poff = jnp.where(sub > 0, cur, prv)
        psub = jnp.where(sub > 0, sub - 1, nsub - 1)
        kpc = jnp.concatenate([rows(kbuf, (), poff, psub, r), rows(kbuf, (), cur, sub, r)], axis=0)
        vt_c = rows(v_ref, (0,), 0, sub, r).astype(F32).T.astype(BF16)
        tcur = (n % 2) * nblk + t
        tprv = jnp.where(sub > 0, tcur - dil, (1 - n % 2) * nblk + (nsub - 1) * dil + r)
        vtbuf[tcur] = vt_c
        vt_p = vtbuf[tprv]
        no_prev = jnp.logical_and(n == 0, sub == 0)
        bias = jnp.where(no_prev, -jnp.inf, 0.0).astype(F32)
        zero_q = jnp.zeros((blk, LANES), BF16)
        zero_p = jnp.zeros((blk, blk), BF16)
        for h in range(ATT_SLOTS):
            c, e = divmod(h, LANES // hd)
            cs = slice(c * LANES, (c + 1) * LANES)
            q_h = jnp.where(lane_head == e, qb[:, cs], zero_q)
            st = lax.dot_general(kpc[:, cs], q_h, nt, preferred_element_type=F32)
            st_p = st[0:blk] + bias
            s_scr[w0 + h] = jnp.where(upper, st_p, st[blk:2 * blk])
            far_scr[w0 + h:w0 + h + 1, :] = jnp.sum(
                jnp.where(diag, st_p, 0.0), axis=0, keepdims=True)
        for h in range(ATT_SLOTS):
            sc = s_scr[w0 + h]
            far = far_scr[w0 + h:w0 + h + 1, :]
            m = jnp.maximum(jnp.max(sc, axis=0, keepdims=True), far)
            p = jnp.exp(sc - m)
            p_far = jnp.exp(far - m)
            inv = 1.0 / (jnp.sum(p, axis=0, keepdims=True) + p_far)
            pb = p.astype(BF16)
            p_scr[w0 + h, 0:blk] = jnp.where(upper, pb, zero_p)
            p_scr[w0 + h, blk:2 * blk] = jnp.where(upper, zero_p, pb)
            far_scr[w0 + h:w0 + h + 1, :] = p_far
            inv_scr[w0 + h:w0 + h + 1, :] = inv
            lt_ref[w, h:h + 1, :] = m - jnp.log(inv)
        o_parts = []
        for h in range(ATT_SLOTS):
            hs = slice(h * hd, (h + 1) * hd)
            vtp_h = vt_p[hs]
            ot = jnp.dot(jnp.concatenate([vtp_h, vt_c[hs]], axis=1), p_scr[w0 + h],
                         preferred_element_type=F32)
            o_parts.append((ot + vtp_h.astype(F32) * far_scr[w0 + h:w0 + h + 1, :])
                           * inv_scr[w0 + h:w0 + h + 1, :])
        start = pl.multiple_of(t * blk, blk)
        operm[pl.ds(start, blk), :] = jnp.concatenate(o_parts, axis=0).T
        lperm[pl.ds(start, blk), :] = lt_ref[w].T

    def block_pair(i, carry):
        for w in range(ATT_STREAMS):
            block(i * ATT_STREAMS + w, w)
        return carry

    lax.fori_loop(0, nblk // ATT_STREAMS, block_pair, 0)

    if dil == 1:
        o_ref[0] = operm[...].astype(o_ref.dtype)
        l_ref[0] = lperm[...]
        return
    nslab = D_ATT // LANES
    for sub in range(nsub):
        for r in range(dil):
            src = (sub * dil + r) * blk
            dst = pl.ds(sub * win + r, blk, stride=dil) if dil > 1 else pl.ds(src, blk)
            for c in range(nslab):
                onat[c, dst, :] = operm[src:src + blk, c * LANES:(c + 1) * LANES]
            l_ref[0, dst, :] = lperm[src:src + blk, :]
    for c in range(nslab):
        o_ref[0, :, c * LANES:(c + 1) * LANES] = onat[c].astype(o_ref.dtype)


def _attn_prompt(u2d, b, s, gi):
    win, dil = ATT_GROUPS[gi]
    sb = ATT_SB
    nsb = s // sb
    assert win // dil == ATT_BLOCK and s == nsb * sb and sb % win == 0
    col0 = (U_ATT + gi * U_TILE) // D_ATT

    def ucol(k):
        return pl.BlockSpec((1, sb, D_ATT), lambda bi, n: (bi, n, col0 + k))

    tok = lambda width: pl.BlockSpec((1, sb, width), lambda bi, n: (bi, n, 0))
    u3 = u2d.reshape(b, s, N_U)
    o, lse = pl.pallas_call(
        functools.partial(_attn_prompt_kernel, win=win, dil=dil),
        grid=(b, nsb),
        in_specs=[ucol(0), ucol(1), ucol(2)],
        out_specs=[tok(D_ATT), tok(LANES)],
        out_shape=[
            jax.ShapeDtypeStruct((b, s, D_ATT), BF16),
            jax.ShapeDtypeStruct((b, s, LANES), F32),
        ],
        scratch_shapes=[
            pltpu.VMEM((2 * sb, D_ATT), BF16), pltpu.VMEM((2 * sb // ATT_BLOCK, D_ATT, ATT_BLOCK), BF16),
            pltpu.VMEM((sb, D_ATT), F32), pltpu.VMEM((D_ATT // LANES, sb, LANES), F32),
            pltpu.VMEM((sb, LANES), F32), pltpu.VMEM((ATT_STREAMS, LANES, ATT_BLOCK), F32),
            pltpu.VMEM((ATT_STREAMS * ATT_SLOTS, ATT_BLOCK, ATT_BLOCK), F32),
            pltpu.VMEM((ATT_STREAMS * ATT_SLOTS, 2 * ATT_BLOCK, ATT_BLOCK), BF16),
            pltpu.VMEM((ATT_STREAMS * ATT_SLOTS, ATT_BLOCK), F32),
            pltpu.VMEM((ATT_STREAMS * ATT_SLOTS, ATT_BLOCK), F32),
        ],
        compiler_params=pltpu.CompilerParams(
            dimension_semantics=("parallel", "arbitrary"), vmem_limit_bytes=VMEM_LIMIT),
        name=f"attn_prompt_g{gi}",
    )(u3, u3, u3)
    return o.reshape(b * s, D_ATT), lse.reshape(b * s, LANES)


def _kv_tail_kernel(x_ref, o_ref, nat_ref, *, win, dil):
    span = min(win, IN_TM)
    per_slab = LANES // ATT_HEAD_DIM
    for c in range(D_ATT // LANES):
        cs = slice(c * LANES, (c + 1) * LANES)
        if dil == 1:
            nat = x_ref[0, :, cs].astype(F32)
        else:
            for t in range(win // span):
                for dst, src, cnt in _residue_pieces(win, dil, span):
                    nat_ref[c, pl.ds(t * span + src, cnt, stride=dil), :] = (
                        x_ref[0, t * span + dst:t * span + dst + cnt, cs].astype(F32))
            nat = nat_ref[c]
        o_ref[0, 0, c * per_slab:(c + 1) * per_slab] = nat.T.reshape(per_slab, ATT_HEAD_DIM, win)


def _kv_tail(u2d, b, s, gi):
    win, dil = ATT_GROUPS[gi]
    assert s % win == 0 and (win <= IN_TM or win % IN_TM == 0)
    col0 = (U_ATT + gi * U_TILE) // D_ATT + 1
    last = s // win - 1
    u3 = u2d.reshape(b, s, N_U)
    return pl.pallas_call(
        functools.partial(_kv_tail_kernel, win=win, dil=dil),
        grid=(b, 2),
        in_specs=[pl.BlockSpec((1, win, D_ATT), lambda bi, kv: (bi, last, col0 + kv))],
        out_specs=pl.BlockSpec((1, 1, ATT_SLOTS, ATT_HEAD_DIM, win), lambda bi, kv: (bi, kv, 0, 0, 0)),
        out_shape=jax.ShapeDtypeStruct((b, 2, ATT_SLOTS, ATT_HEAD_DIM, win), F32),
        scratch_shapes=[pltpu.VMEM((D_ATT // LANES, win, LANES), F32)],
        compiler_params=pltpu.CompilerParams(
            dimension_semantics=("parallel", "parallel"), vmem_limit_bytes=VMEM_LIMIT),
        name=f"kv_tail_g{gi}",
    )(u3)


def _attn_decode_kernel(x_ref, c0_ref, c1_ref, c2_ref, o_ref, n0_ref, n1_ref, n2_ref):
    caches = (c0_ref, c1_ref, c2_ref)
    news = (n0_ref, n1_ref, n2_ref)
    ng = N_ATT_GROUPS

    for hh in range(x_ref.shape[1]):
        outs_g, lses = [], []
        for g, (win, dil) in enumerate(ATT_GROUPS):
            qc = x_ref[0, hh, g]
            kc = x_ref[0, hh, ng + g]
            vc = x_ref[0, hh, 2 * ng + g]
            kt = caches[g][0, 0, hh]
            vt = caches[g][0, 1, hh]
            lane = lax.broadcasted_iota(jnp.int32, (1, win), 1)
            sc = jnp.sum(kt * qc, axis=0, keepdims=True)
            sc = jnp.where(lane % dil == 0, sc, -jnp.inf)
            sc_new = jnp.sum(kc * qc, axis=0, keepdims=True)
            m = jnp.maximum(jnp.max(sc, axis=1, keepdims=True), sc_new)
            p = jnp.exp(sc - m)
            p_new = jnp.exp(sc_new - m)
            den = jnp.sum(p, axis=1, keepdims=True) + p_new
            outs_g.append((jnp.sum(vt * p, axis=1, keepdims=True) + p_new * vc) / den)
            lses.append(m + jnp.log(den))
            last = lane == win - 1
            news[g][0, 0, hh] = jnp.where(last, kc, pltpu.roll(kt, win - 1, axis=1))
            news[g][0, 1, hh] = jnp.where(last, vc, pltpu.roll(vt, win - 1, axis=1))

        top = jnp.maximum(jnp.maximum(lses[0], lses[1]), lses[2])
        ws = [jnp.exp(l - top) for l in lses]
        o_ref[0, hh] = ((ws[0] * outs_g[0] + ws[1] * outs_g[1] + ws[2] * outs_g[2])
                        / (ws[0] + ws[1] + ws[2]))


def _attn_decode(us, caches):
    b = us.shape[0]
    ng = N_ATT_GROUPS
    hd = ATT_HEAD_DIM
    hb = DEC_HEADS_PER_STEP
    x = us[:, U_ATT:].reshape(b, ng, 3, ATT_SLOTS, hd)
    x = x.transpose(0, 3, 2, 1, 4).reshape(b, ATT_SLOTS, 3 * ng, hd, 1)
    cts = [c.transpose(0, 2, 3, 4, 1) for c in caches]
    cache_specs = []
    for c, (win, dil) in zip(cts, ATT_GROUPS):
        assert c.shape == (b, 2, ATT_SLOTS, hd, win) and win // dil == ATT_BLOCK
        cache_specs.append(pl.BlockSpec((1, 2, hb, hd, win), lambda bi, h: (bi, 0, h, 0, 0)))
    res = pl.pallas_call(
        _attn_decode_kernel,
        grid=(b, ATT_SLOTS // hb),
        in_specs=[pl.BlockSpec((1, hb, 3 * ng, hd, 1), lambda bi, h: (bi, h, 0, 0, 0))] + cache_specs,
        out_specs=[pl.BlockSpec((1, hb, hd, 1), lambda bi, h: (bi, h, 0, 0))] + cache_specs,
        out_shape=[jax.ShapeDtypeStruct((b, ATT_SLOTS, hd, 1), F32)]
        + [jax.ShapeDtypeStruct(c.shape, F32) for c in cts],
        compiler_params=pltpu.CompilerParams(
            dimension_semantics=("parallel", "parallel"), vmem_limit_bytes=VMEM_LIMIT),
        name="attn_decode",
    )(x, *cts)
    o = res[0].reshape(b, D_ATT)
    kvs = [r.transpose(0, 4, 1, 2, 3) for r in res[1:]]
    return o, kvs


def _out_kernel(*refs, n_groups):
    o_refs = refs[:n_groups]
    l_refs = refs[n_groups:2 * n_groups - (n_groups == 1)]
    z_ref, br_ref, gs_ref, ga_ref, x_ref, he_ref, wa_ref, wo_ref, y_ref = refs[len(o_refs) + len(l_refs):]
    if n_groups == 1:
        o = o_refs[0][...].astype(F32)
    else:
        ls = [l[...] for l in l_refs]
        top = functools.reduce(jnp.maximum, ls)
        ws = [jnp.exp(l - top) for l in ls]
        den = functools.reduce(lambda a, c: a + c, ws)
        o = None
        for w, o_ref in zip(ws, o_refs):
            term = _split_dot(w / den, he_ref[...], 2) * o_ref[...].astype(F32)
            o = term if o is None else o + term
    o = o * _silu(z_ref[...].astype(F32))
    br_att = jnp.dot(o.astype(BF16), wa_ref[...], preferred_element_type=F32)
    mix = (_sigmoid(gs_ref[...].astype(F32)) * br_ref[...].astype(F32)
           + _sigmoid(ga_ref[...].astype(F32)) * br_att)
    y_ref[...] = x_ref[...] + jnp.dot(mix.astype(BF16), wo_ref[...], preferred_element_type=F32)


def _out_proj(os_, lses, u2d, br_ssd, x2d, p, tm):
    m = x2d.shape[0]
    row = lambda width, off=0: pl.BlockSpec((tm, width), lambda i: (i, off // width))
    return pl.pallas_call(
        functools.partial(_out_kernel, n_groups=len(os_)),
        grid=(m // tm,),
        in_specs=[row(D_ATT)] * len(os_) + [row(LANES)] * len(lses)
        + [row(D_ATT, U_ZATT), row(D_MODEL), row(D_MODEL, U_GSSD), row(D_MODEL, U_GATT), row(D_MODEL),
           _const_spec((LANES, D_ATT)), _const_spec((D_ATT, D_MODEL)), _const_spec((D_MODEL, D_MODEL))],
        out_specs=row(D_MODEL),
        out_shape=jax.ShapeDtypeStruct((m, D_MODEL), F32),
        compiler_params=pltpu.CompilerParams(
            dimension_semantics=("parallel",), vmem_limit_bytes=VMEM_LIMIT),
        name="out_proj",
    )(*os_, *lses, u2d, br_ssd, u2d, u2d, x2d, p["slot_expand"], p["w_att_br"], p["w_out"])


def _rope_tables(pos):
    half = ATT_HEAD_DIM // 2
    inv = ROPE_THETA ** (-jnp.arange(half, dtype=F32) / half)
    ang = pos.astype(F32)[:, None] * inv[None, :]
    cos, sin = jnp.cos(ang), jnp.sin(ang)
    cos = jnp.concatenate([cos, cos], axis=-1)
    sin = jnp.concatenate([-sin, sin], axis=-1)
    return jnp.tile(cos, (1, LANES // ATT_HEAD_DIM)), jnp.tile(sin, (1, LANES // ATT_HEAD_DIM))


def _storage_positions(s, tm):
    out = []
    for win, dil in ATT_GROUPS:
        idx = np.arange(s)
        if dil > 1:
            for t0 in range(0, s, tm):
                for dst, src, cnt in _residue_pieces(win, dil, tm):
                    idx[t0 + dst:t0 + dst + cnt] = t0 + src + dil * np.arange(cnt)
        out.append(idx)
    return np.stack(out)


def _layer_params(l, norm_w, w_in, conv_w, conv_b, dt_bias, a_log, d_skip, ssd_norm_w, q_norm_w,
                  k_norm_w, w_ssd_br, w_att_br, w_out):
    w = w_in[l]
    o_dt = D_SSD + D_XBC
    o_q = o_dt + SSD_HEADS
    o_k, o_v = o_q + D_ATT_QK, o_q + 2 * D_ATT_QK
    o_zatt = o_q + 3 * D_ATT_QK
    o_gssd, o_gatt = o_zatt + D_ATT, o_zatt + D_ATT + D_MODEL
    cols = [w[:, :o_dt], w[:, o_gssd:o_gatt], w[:, o_gatt:o_gatt + D_MODEL], w[:, o_zatt:o_gssd]]
    for g in range(N_ATT_GROUPS):
        cols += [w[:, o + g * D_ATT:o + (g + 1) * D_ATT] for o in (o_q, o_k, o_v)]
    pad = LANES - SSD_HEADS
    lane_i = np.arange(LANES)
    return dict(
        norm_w=norm_w[l][None, :],
        w_main=jnp.concatenate(cols, axis=1).astype(BF16),
        w_dt=jnp.pad(w[:, o_dt:o_q], ((0, 0), (0, pad))).astype(BF16),
        conv_w=conv_w[l], conv_b=conv_b[l][None, :],
        dt_bias=jnp.pad(dt_bias[l].astype(F32), (0, pad))[None, :],
        a_neg=jnp.pad(-jnp.exp(a_log[l].astype(F32)), (0, pad))[None, :],
        d_skip=jnp.repeat(d_skip[l].astype(F32), SSD_HEAD_DIM)[None, :],
        ssd_norm_w=ssd_norm_w[l][None, :],
        head_expand=jnp.asarray(
            lane_i[:, None] == (np.arange(D_SSD) // SSD_HEAD_DIM)[None, :], BF16),
        slot_expand=jnp.asarray(
            lane_i[:, None] == (np.arange(D_ATT) // ATT_HEAD_DIM)[None, :], BF16),
        head_seg=jnp.asarray(
            lane_i[:, None] // ATT_HEAD_DIM == lane_i[None, :] // ATT_HEAD_DIM, BF16),
        qk_norm_w=jnp.concatenate([jnp.tile(q_norm_w[l].astype(F32) * ATT_SCALE, ATT_SLOTS),
                                   jnp.tile(k_norm_w[l].astype(F32), ATT_SLOTS)])[None, :],
        w_ssd_br=w_ssd_br[l].astype(BF16), w_att_br=w_att_br[l].astype(BF16),
        w_out=w_out[l].astype(BF16),
    )


def _prompt_layer(x, p):
    b, s, _ = x.shape
    assert s % ATT_SB == 0 and ATT_SB % IN_TM == 0
    x2d = x.reshape(b * s, D_MODEL)
    order = jnp.asarray(_storage_positions(s, IN_TM).reshape(-1), jnp.int32)
    cos, sin = _rope_tables(order)
    shape3 = (N_ATT_GROUPS, s, LANES)
    u, dt = _inproj(x2d, p, cos.reshape(shape3), sin.reshape(shape3), BF16, IN_TM, True)
    conv0 = jnp.zeros((b, CONV_WIDTH - 1, D_XBC), F32)
    ssm0 = jnp.zeros((b, SSD_HEADS, SSD_HEAD_DIM, SSD_STATE), F32)
    br_ssd, conv_new, ssm_new = _ssd(u.reshape(b, s, N_U), dt.reshape(b, s, LANES), conv0, ssm0, p, BF16)
    os_, lses, kvs = [], [], []
    for gi in range(N_ATT_GROUPS):
        o, lse = _attn_prompt(u, b, s, gi)
        os_.append(o)
        lses.append(lse)
        kvs.append(_kv_tail(u, b, s, gi).transpose(0, 4, 1, 2, 3))
    y = _out_proj(os_, lses, u, br_ssd.reshape(b * s, D_MODEL), x2d, p, 512)
    return y.reshape(b, s, D_MODEL), conv_new, ssm_new, kvs


def _decode_layer(x, conv_state, ssm_state, caches, p):
    b, s, _ = x.shape
    assert s == 1
    x2d = x.reshape(b, D_MODEL)
    cos, sin = _rope_tables(jnp.full((b,), PAST_LEN, jnp.int32))
    cos3 = jnp.broadcast_to(cos[None], (N_ATT_GROUPS, b, LANES))
    sin3 = jnp.broadcast_to(sin[None], (N_ATT_GROUPS, b, LANES))
    u, dt = _inproj(x2d, p, cos3, sin3, F32, b, False)
    br_ssd, conv_new, ssm_new = _ssd(u.reshape(b, s, N_U), dt.reshape(b, s, LANES),
                                     conv_state, ssm_state, p, F32)
    o, kvs = _attn_decode(u, caches)
    y = _out_proj([o], [], u, br_ssd.reshape(b, D_MODEL), x2d, p, b)
    return y.reshape(b, s, D_MODEL), conv_new, ssm_new, kvs


def kernel(x_prompt, x_sample, state_conv, state_ssm, cache_kv_w128, cache_kv_w512, cache_kv_w2048,
           norm_w, w_in, conv_w, conv_b, dt_bias, a_log, d_skip, ssd_norm_w, q_norm_w, k_norm_w,
           w_ssd_br, w_att_br, w_out):
    yp, ys = x_prompt, x_sample
    outs_p = [[] for _ in range(5)]
    outs_s = [[] for _ in range(5)]
    for l in range(norm_w.shape[0]):
        p = _layer_params(l, norm_w, w_in, conv_w, conv_b, dt_bias, a_log, d_skip, ssd_norm_w,
                          q_norm_w, k_norm_w, w_ssd_br, w_att_br, w_out)
        yp, c, h, kv = _prompt_layer(yp, p)
        for acc, val in zip(outs_p, [c, h] + kv):
            acc.append(val)
        ys, c, h, kv = _decode_layer(ys, state_conv[l], state_ssm[l],
                                     [cache_kv_w128[l], cache_kv_w512[l], cache_kv_w2048[l]], p)
        for acc, val in zip(outs_s, [c, h] + kv):
            acc.append(val)
    return (yp, ys, *[jnp.stack(a) for a in outs_p], *[jnp.stack(a) for a in outs_s])
```

```python
import functools

import numpy as np
import jax
import jax.numpy as jnp
from jax import lax
from jax.experimental import pallas as pl
from jax.experimental.pallas import tpu as pltpu

F32 = jnp.float32
BF16 = jnp.bfloat16

D_MODEL = 1024
D_SSD = 2048
SSD_HEADS = 32
SSD_HEAD_DIM = 64
SSD_GROUPS = 4
SSD_STATE = 128
CONV_WIDTH = 4
D_BC = SSD_GROUPS * SSD_STATE
D_XBC = D_SSD + 2 * D_BC
ATT_HEAD_DIM = 64
ATT_SLOTS = 8
ATT_GROUPS = ((128, 1), (512, 4), (2048, 16))
N_ATT_GROUPS = len(ATT_GROUPS)
D_ATT = ATT_SLOTS * ATT_HEAD_DIM
D_ATT_QK = N_ATT_GROUPS * D_ATT
ATT_SCALE = ATT_HEAD_DIM ** -0.5
ROPE_THETA = 10000.0
EPS = 1e-6
PAST_LEN = 16384

U_Z, U_X, U_B, U_C, U_GSSD, U_GATT, U_ZATT = 0, 2048, 4096, 4608, 5120, 6144, 7168
U_ATT = 7680
U_TILE = 3 * D_ATT
N_U = U_ATT + N_ATT_GROUPS * U_TILE
N_PLAIN_TILES = U_ATT // U_TILE

LANES = 128
SUBLANES = 8
SSD_CHUNK = 128
SSD_DECODE_CHUNK = 16
SSD_CHUNKS_PER_STEP = 4
ATT_BLOCK = 128
ATT_SB = 2048
ATT_STREAMS = 8
IN_TM = 1024
DEC_HEADS_PER_STEP = 4
VMEM_LIMIT = 56 * 1024 * 1024


def _split_dot(a, m, terms, left=False):
    out = None
    r = a
    for t in range(terms):
        p = r.astype(BF16)
        d = (jnp.dot(m, p, preferred_element_type=F32) if left
             else jnp.dot(p, m, preferred_element_type=F32))
        out = d if out is None else out + d
        if t + 1 < terms:
            r = r - p.astype(F32)
    return out


def _sigmoid(x):
    return 1.0 / (1.0 + jnp.exp(-x))


def _silu(x):
    return x * _sigmoid(x)


def _softplus(x):
    return jnp.maximum(x, 0.0) + jnp.log1p(jnp.exp(-jnp.abs(x)))


def _const_spec(shape):
    return pl.BlockSpec(shape, lambda *_: (0,) * len(shape))


def _residue_pieces(win, dil, tm):
    span = min(win, tm)
    per = span // dil
    return [(w * span + r * per, w * span + r, per)
            for w in range(max(tm // win, 1)) for r in range(dil)]


def _inproj_kernel(x_ref, nw_ref, w_ref, wdt_ref, cos_ref, sin_ref, qkw_ref, seg_ref,
                   u_ref, dt_ref, hn_ref, *perm_refs, tm, permute):
    j = pl.program_id(1)

    @pl.when(j == 0)
    def _():
        x = x_ref[...]
        ms = jnp.mean(x * x, axis=-1, keepdims=True)
        hn = x * lax.rsqrt(ms + EPS) * nw_ref[...]
        hn_ref[0] = hn.astype(BF16)
        dt_ref[...] = jnp.dot(hn_ref[0], wdt_ref[...], preferred_element_type=F32)
        if permute:
            (hnf_ref,) = perm_refs
            for c in range(D_MODEL // LANES):
                hnf_ref[c] = hn[:, c * LANES:(c + 1) * LANES]
            for g, (win, dil) in enumerate(ATT_GROUPS):
                if dil == 1:
                    continue
                for dst, src, cnt in _residue_pieces(win, dil, tm):
                    for c in range(D_MODEL // LANES):
                        hn_ref[g, dst:dst + cnt, c * LANES:(c + 1) * LANES] = (
                            hnf_ref[c, pl.ds(src, cnt, stride=dil), :].astype(BF16))

    @pl.when(j < N_PLAIN_TILES)
    def _():
        u_ref[...] = jnp.dot(hn_ref[0], w_ref[...], preferred_element_type=F32).astype(u_ref.dtype)

    half = ATT_HEAD_DIM // 2
    for g, (_, dil) in enumerate(ATT_GROUPS):
        @pl.when(j == N_PLAIN_TILES + g)
        def _(g=g, dil=dil):
            src = g if (permute and dil > 1) else 0
            acc = jnp.dot(hn_ref[src], w_ref[...], preferred_element_type=F32)
            cos, sin, seg = cos_ref[0], sin_ref[0], seg_ref[...]
            lane = lax.broadcasted_iota(jnp.int32, (tm, LANES), 1)
            first = lane % ATT_HEAD_DIM < half
            for c in range(2 * D_ATT // LANES):
                cs = slice(c * LANES, (c + 1) * LANES)
                xs = acc[:, cs]
                ss = _split_dot(xs * xs, seg, 2)
                xn = xs * lax.rsqrt(ss * (1.0 / ATT_HEAD_DIM) + EPS) * qkw_ref[:, cs]
                partner = jnp.where(first, pltpu.roll(xn, LANES - half, axis=1),
                                    pltpu.roll(xn, half, axis=1))
                u_ref[:, cs] = (xn * cos + partner * sin).astype(u_ref.dtype)
            u_ref[:, 2 * D_ATT:] = acc[:, 2 * D_ATT:].astype(u_ref.dtype)


def _inproj(x2d, p, cos3, sin3, out_dtype, tm, permute):
    m = x2d.shape[0]
    period = cos3.shape[1] // tm
    tab = pl.BlockSpec((1, tm, LANES),
                       lambda i, j: (jnp.clip(j - N_PLAIN_TILES, 0, N_ATT_GROUPS - 1), i % period, 0))
    scratch = [pltpu.VMEM((N_ATT_GROUPS if permute else 1, tm, D_MODEL), BF16)]
    if permute:
        scratch.append(pltpu.VMEM((D_MODEL // LANES, tm, LANES), F32))
    return pl.pallas_call(
        functools.partial(_inproj_kernel, tm=tm, permute=permute),
        grid=(m // tm, N_U // U_TILE),
        in_specs=[
            pl.BlockSpec((tm, D_MODEL), lambda i, j: (i, 0)),
            _const_spec((1, D_MODEL)),
            pl.BlockSpec((D_MODEL, U_TILE), lambda i, j: (0, j)),
            _const_spec((D_MODEL, LANES)),
            tab, tab,
            _const_spec((1, 2 * D_ATT)),
            _const_spec((LANES, LANES)),
        ],
        out_specs=[
            pl.BlockSpec((tm, U_TILE), lambda i, j: (i, j)),
            pl.BlockSpec((tm, LANES), lambda i, j: (i, 0)),
        ],
        out_shape=[
            jax.ShapeDtypeStruct((m, N_U), out_dtype),
            jax.ShapeDtypeStruct((m, LANES), F32),
        ],
        scratch_shapes=scratch,
        compiler_params=pltpu.CompilerParams(
            dimension_semantics=("parallel", "arbitrary"), vmem_limit_bytes=VMEM_LIMIT),
        name="inproj",
    )(x2d, p["norm_w"], p["w_main"], p["w_dt"], cos3, sin3, p["qk_norm_w"], p["head_seg"])


def _ssd_kernel(z_ref, x_ref, bm_ref, cm_ref, dt_ref, cs_ref, h0_ref, cw_ref, cb_ref, dtb_ref,
                a_ref, dsk_ref, nw_ref, e_ref, sh_ref, wbr_ref, y_ref, cso_ref, ho_ref,
                xin_ref, dtin_ref, ht_ref, yd_ref, *pad_refs, q, t, nc, nsub):
    c = pl.program_id(1)
    tail = CONV_WIDTH - 1
    base = SUBLANES
    hp = D_SSD
    gw = D_SSD // SSD_GROUPS
    e_per_g = SSD_HEADS // SSD_GROUPS
    full = t == q

    @pl.when(c == 0)
    def _():
        xin_ref[0:2 * base if full else base, :] = jnp.zeros((2 * base if full else base, D_XBC), F32)
        xin_ref[base - tail:base, :] = cs_ref[0]
        ht_ref[...] = h0_ref[0].reshape(hp, SSD_STATE).T

    if not full:
        xin_ref[base:base + q, :] = jnp.zeros((q, D_XBC), F32)
        dtin_ref[...] = jnp.zeros((q, LANES), F32)
        (zin_ref,) = pad_refs
        zin_ref[...] = jnp.zeros((q, D_SSD), F32)
        zin_ref[0:t, :] = z_ref[0].astype(F32)
        xin_ref[base:base + t, 0:D_SSD] = x_ref[0].astype(F32)
        xin_ref[base:base + t, D_SSD:D_SSD + D_BC] = bm_ref[0].astype(F32)
        xin_ref[base:base + t, D_SSD + D_BC:D_XBC] = cm_ref[0].astype(F32)
        dtin_ref[0:t, :] = dt_ref[0]

    def conv(src_ref, r0, off, lo, hi):
        if not full:
            acc = cb_ref[:, lo:hi]
            for j in range(CONV_WIDTH):
                acc = acc + cw_ref[j:j + 1, lo:hi] * xin_ref[base - tail + j:base - tail + j + q, lo:hi]
            return _silu(acc)
        xb = src_ref[0, r0:r0 + q, off:off + hi - lo]
        xf = xb.astype(F32)
        shifted = jnp.dot(sh_ref[...], xb, preferred_element_type=F32)
        acc = cb_ref[:, lo:hi] + cw_ref[tail:tail + 1, lo:hi] * xf
        head = acc[0:base]
        for j in range(tail):
            acc = acc + cw_ref[j:j + 1, lo:hi] * shifted[j * q:(j + 1) * q]
            head = head + cw_ref[j:j + 1, lo:hi] * (
                shifted[j * q:j * q + base] + xin_ref[base - tail + j:2 * base - tail + j, lo:hi])
        xin_ref[0:base, lo:hi] = xf[q - base:q]
        return _silu(jnp.concatenate([head, acc[base:]], axis=0))

    ri = lax.broadcasted_iota(jnp.int32, (q, q), 0)
    ci = lax.broadcasted_iota(jnp.int32, (q, q), 1)
    causal = ri >= ci
    tril = jnp.where(causal, 1.0, 0.0).astype(BF16)
    e_mat = e_ref[...]
    heads_per_slab = LANES // SSD_HEAD_DIM
    slab_head = lax.broadcasted_iota(jnp.int32, (q, LANES), 1) // SSD_HEAD_DIM

    def chunk(sub):
        r0 = sub * q
        dt = _softplus((dt_ref[0, r0:r0 + q, :] if full else dtin_ref[...]) + dtb_ref[...])
        if not full:
            rows = lax.broadcasted_iota(jnp.int32, (q, LANES), 0)
            dt = jnp.where(rows < t, dt, 0.0)
        a = dt * a_ref[...]
        acum = _split_dot(a, tril, 3, left=True)
        acum_t = acum.T
        a_last = acum[q - 1:q, :]
        dt_t = dt.T
        ea_e = _split_dot(jnp.exp(acum), e_mat, 2)
        dtds_e = _split_dot(dt * jnp.exp(a_last - acum), e_mat, 2)

        bmat = conv(bm_ref, r0, 0, D_SSD, D_SSD + D_BC)
        cmat = conv(cm_ref, r0, 0, D_SSD + D_BC, D_XBC).astype(BF16)
        bt = bmat.T.astype(BF16)

        yn_parts = []
        for g in range(SSD_GROUPS):
            gs = slice(g * gw, (g + 1) * gw)
            xs = conv(x_ref, r0, g * gw, g * gw, (g + 1) * gw)
            xs_b = xs.astype(BF16)
            cg = cmat[:, g * SSD_STATE:(g + 1) * SSD_STATE]
            bgt = bt[g * SSD_STATE:(g + 1) * SSD_STATE, :]
            cb = jnp.dot(cg, bgt, preferred_element_type=F32)
            ht_g = ht_ref[:, gs]
            y_off = jnp.dot(cg, ht_g.astype(BF16), preferred_element_type=F32)
            for e in range(0, e_per_g, heads_per_slab):
                mats = []
                for h in range(g * e_per_g + e, g * e_per_g + e + heads_per_slab):
                    seg = acum[:, h:h + 1] - acum_t[h:h + 1, :]
                    lmat = jnp.exp(jnp.where(causal, seg, -jnp.inf))
                    mats.append((cb * lmat * dt_t[h:h + 1, :]).astype(BF16))
                ls = slice(e * SSD_HEAD_DIM, e * SSD_HEAD_DIM + LANES)
                slab = xs_b[:, ls]
                diag = jnp.concatenate(
                    [jnp.where(slab_head == i, slab, jnp.zeros_like(slab)) for i in range(heads_per_slab)],
                    axis=0)
                yd_ref[sub, :, g * gw + ls.start:g * gw + ls.stop] = jnp.dot(
                    jnp.concatenate(mats, axis=1), diag, preferred_element_type=F32)
            y = yd_ref[sub, :, gs] + y_off * ea_e[:, gs] + dsk_ref[:, gs] * xs
            xw = (xs * dtds_e[:, gs]).astype(BF16)
            ht_ref[:, gs] = ht_g * ea_e[q - 1:q, gs] + jnp.dot(bgt, xw, preferred_element_type=F32)
            zg = z_ref[0, r0:r0 + q, gs].astype(F32) if full else zin_ref[:, gs]
            yg = y * _silu(zg)
            ms = jnp.mean(yg * yg, axis=-1, keepdims=True)
            yn_parts.append((yg * lax.rsqrt(ms + EPS) * nw_ref[:, gs]).astype(BF16))
        yn = jnp.concatenate(yn_parts, axis=-1)
        br = jnp.dot(yn, wbr_ref[...], preferred_element_type=F32)
        y_ref[0, r0:r0 + t, :] = br[0:t, :].astype(y_ref.dtype)

    for sub in range(nsub):
        chunk(sub)

    @pl.when(c == nc - 1)
    def _():
        last = base if full else base + t
        cso_ref[0] = xin_ref[last - tail:last, :]
        ho_ref[0] = ht_ref[...].T.reshape(SSD_HEADS, SSD_HEAD_DIM, SSD_STATE)


def _ssd(u3, dt3, conv_state, ssm_state, p, out_dtype):
    b, s, _ = u3.shape
    q = SSD_CHUNK if s >= SSD_CHUNK else SSD_DECODE_CHUNK
    t = min(q, s)
    nsub = SSD_CHUNKS_PER_STEP if s % (SSD_CHUNKS_PER_STEP * q) == 0 else 1
    rows = nsub * t
    nc = s // rows
    assert s == nc * rows and (t == q or nc == 1)

    def ucol(width, off):
        return pl.BlockSpec((1, rows, width), lambda bi, ci: (bi, ci, off // width))

    tail = CONV_WIDTH - 1
    ti = np.arange(q)
    shift = np.concatenate([ti[None, :] == (ti[:, None] - (tail - j)) for j in range(tail)], axis=0)
    scratch = [
        pltpu.VMEM((2 * SUBLANES if t == q else SUBLANES + q, D_XBC), F32),
        pltpu.VMEM((q, LANES), F32),
        pltpu.VMEM((SSD_STATE, D_SSD), F32),
        pltpu.VMEM((nsub, q, D_SSD), F32),
    ]
    if t < q:
        scratch.append(pltpu.VMEM((q, D_SSD), F32))
    return pl.pallas_call(
        functools.partial(_ssd_kernel, q=q, t=t, nc=nc, nsub=nsub),
        grid=(b, nc),
        in_specs=[
            ucol(D_SSD, U_Z), ucol(D_SSD, U_X), ucol(D_BC, U_B), ucol(D_BC, U_C),
            pl.BlockSpec((1, rows, LANES), lambda bi, ci: (bi, ci, 0)),
            pl.BlockSpec((1, CONV_WIDTH - 1, D_XBC), lambda bi, ci: (bi, 0, 0)),
            pl.BlockSpec((1, SSD_HEADS, SSD_HEAD_DIM, SSD_STATE), lambda bi, ci: (bi, 0, 0, 0)),
            _const_spec((CONV_WIDTH, D_XBC)), _const_spec((1, D_XBC)),
            _const_spec((1, LANES)), _const_spec((1, LANES)),
            _const_spec((1, D_SSD)), _const_spec((1, D_SSD)),
            _const_spec((LANES, D_SSD)), _const_spec((tail * q, q)), _const_spec((D_SSD, D_MODEL)),
        ],
        out_specs=[
            pl.BlockSpec((1, rows, D_MODEL), lambda bi, ci: (bi, ci, 0)),
            pl.BlockSpec((1, CONV_WIDTH - 1, D_XBC), lambda bi, ci: (bi, 0, 0)),
            pl.BlockSpec((1, SSD_HEADS, SSD_HEAD_DIM, SSD_STATE), lambda bi, ci: (bi, 0, 0, 0)),
        ],
        out_shape=[
            jax.ShapeDtypeStruct((b, s, D_MODEL), out_dtype),
            jax.ShapeDtypeStruct((b, CONV_WIDTH - 1, D_XBC), F32),
            jax.ShapeDtypeStruct((b, SSD_HEADS, SSD_HEAD_DIM, SSD_STATE), F32),
        ],
        scratch_shapes=scratch,
        compiler_params=pltpu.CompilerParams(
            dimension_semantics=("parallel", "arbitrary"), vmem_limit_bytes=VMEM_LIMIT),
        name="ssd",
    )(u3, u3, u3, u3, dt3, conv_state, ssm_state, p["conv_w"], p["conv_b"], p["dt_bias"], p["a_neg"],
      p["d_skip"], p["ssd_norm_w"], p["head_expand"], jnp.asarray(shift, BF16), p["w_ssd_br"])


def _block_pieces(win, dil):
    if win <= IN_TM:
        return [(0, win, ATT_BLOCK, ATT_BLOCK)]
    per = IN_TM // dil
    return [(t * IN_TM, 0, per, per) for t in range(win // IN_TM)]


def _attn_prompt_kernel(q_ref, k_ref, v_ref, o_ref, l_ref, kbuf, vtbuf, operm, onat, lperm, lt_ref,
                        s_scr, p_scr, far_scr, inv_scr, *, win, dil):
    n = pl.program_id(1)
    sb, blk, hd = ATT_SB, ATT_BLOCK, ATT_HEAD_DIM
    nsub = sb // win
    nblk = sb // blk
    pieces = _block_pieces(win, dil)
    cur = pl.multiple_of((n % 2) * sb, sb)
    prv = pl.multiple_of(sb - (n % 2) * sb, sb)
    kbuf[pl.ds(cur, sb), :] = k_ref[0]

    @pl.when(n == 0)
    def _():
        kbuf[pl.ds(prv, sb), :] = jnp.zeros((sb, D_ATT), BF16)
        vtbuf[pl.ds(nblk, nblk)] = jnp.zeros((nblk, D_ATT, blk), BF16)

    lt_ref[...] = jnp.zeros((ATT_STREAMS, LANES, blk), F32)
    ki = lax.broadcasted_iota(jnp.int32, (blk, blk), 0)
    qi = lax.broadcasted_iota(jnp.int32, (blk, blk), 1)
    upper = ki > qi
    diag = ki == qi
    lane_head = lax.broadcasted_iota(jnp.int32, (blk, LANES), 1) // hd
    nt = (((1,), (1,)), ((), ()))

    def rows(ref, lead, off, sub, r):
        parts = []
        for base, per_sub, per_r, cnt in pieces:
            start = pl.multiple_of(off + base + sub * per_sub + r * per_r, cnt)
            idx = (pl.ds(start, cnt), slice(None))
            parts.append(ref[lead + idx] if lead else ref[idx])
        return parts[0] if len(parts) == 1 else jnp.concatenate(parts, axis=0)

    def block(t, w):
        w0 = w * ATT_SLOTS
        sub, r = t // dil, t % dil
        qb = rows(q_ref, (0,), 0, sub, r)
        poff = jnp.where(sub > 0, cur, prv)
        psub = jnp.where(sub > 0, sub - 1, nsub - 1)
        kpc = jnp.concatenate([rows(kbuf, (), poff, psub, r), rows(kbuf, (), cur, sub, r)], axis=0)
        vt_c = rows(v_ref, (0,), 0, sub, r).astype(F32).T.astype(BF16)
        tcur = (n % 2) * nblk + t
        tprv = jnp.where(sub > 0, tcur - dil, (1 - n % 2) * nblk + (nsub - 1) * dil + r)
        vtbuf[tcur] = vt_c
        vt_p = vtbuf[tprv]
        no_prev = jnp.logical_and(n == 0, sub == 0)
        bias = jnp.where(no_prev, -jnp.inf, 0.0).astype(F32)
        zero_q = jnp.zeros((blk, LANES), BF16)
        zero_p = jnp.zeros((blk, blk), BF16)
        for h in range(ATT_SLOTS):
            c, e = divmod(h, LANES // hd)
            cs = slice(c * LANES, (c + 1) * LANES)
            q_h = jnp.where(lane_head == e, qb[:, cs], zero_q)
            st = lax.dot_general(kpc[:, cs], q_h, nt, preferred_element_type=F32)
            st_p = st[0:blk] + bias
            s_scr[w0 + h] = jnp.where(upper, st_p, st[blk:2 * blk])
            far_scr[w0 + h:w0 + h + 1, :] = jnp.sum(
                jnp.where(diag, st_p, 0.0), axis=0, keepdims=True)
        for h in range(ATT_SLOTS):
            sc = s_scr[w0 + h]
            far = far_scr[w0 + h:w0 + h + 1, :]
            m = jnp.maximum(jnp.max(sc, axis=0, keepdims=True), far)
            p = jnp.exp(sc - m)
            p_far = jnp.exp(far - m)
            inv = 1.0 / (jnp.sum(p, axis=0, keepdims=True) + p_far)
            pb = p.astype(BF16)
            p_scr[w0 + h, 0:blk] = jnp.where(upper, pb, zero_p)
            p_scr[w0 + h, blk:2 * blk] = jnp.where(upper, zero_p, pb)
            far_scr[w0 + h:w0 + h + 1, :] = p_far
            inv_scr[w0 + h:w0 + h + 1, :] = inv
            lt_ref[w, h:h + 1, :] = m - jnp.log(inv)
        o_parts = []
        for h in range(ATT_SLOTS):
            hs = slice(h * hd, (h + 1) * hd)
            vtp_h = vt_p[hs]
            ot = jnp.dot(jnp.concatenate([vtp_h, vt_c[hs]], axis=1), p_scr[w0 + h],
                         preferred_element_type=F32)
            o_parts.append((ot + vtp_h.astype(F32) * far_scr[w0 + h:w0 + h + 1, :])
                           * inv_scr[w0 + h:w0 + h + 1, :])
        start = pl.multiple_of(t * blk, blk)
        operm[pl.ds(start, blk), :] = jnp.concatenate(o_parts, axis=0).T
        lperm[pl.ds(start, blk), :] = lt_ref[w].T

    def block_pair(i, carry):
        for w in range(ATT_STREAMS):
            block(i * ATT_STREAMS + w, w)
        return carry

    lax.fori_loop(0, nblk // ATT_STREAMS, block_pair, 0)

    if dil == 1:
        o_ref[0] = operm[...].astype(o_ref.dtype)
        l_ref[0] = lperm[...]
        return
    nslab = D_ATT // LANES
    for sub in range(nsub):
        for r in range(dil):
            src = (sub * dil + r) * blk
            dst = pl.ds(sub * win + r, blk, stride=dil) if dil > 1 else pl.ds(src, blk)
            for c in range(nslab):
                onat[c, dst, :] = operm[src:src + blk, c * LANES:(c + 1) * LANES]
            l_ref[0, dst, :] = lperm[src:src + blk, :]
    for c in range(nslab):
        o_ref[0, :, c * LANES:(c + 1) * LANES] = onat[c].astype(o_ref.dtype)


def _attn_prompt(u2d, b, s, gi):
    win, dil = ATT_GROUPS[gi]
    sb = ATT_SB
    nsb = s // sb
    assert win // dil == ATT_BLOCK and s == nsb * sb and sb % win == 0
    col0 = (U_ATT + gi * U_TILE) // D_ATT

    def ucol(k):
        return pl.BlockSpec((1, sb, D_ATT), lambda bi, n: (bi, n, col0 + k))

    tok = lambda width: pl.BlockSpec((1, sb, width), lambda bi, n: (bi, n, 0))
    u3 = u2d.reshape(b, s, N_U)
    o, lse = pl.pallas_call(
        functools.partial(_attn_prompt_kernel, win=win, dil=dil),
        grid=(b, nsb),
        in_specs=[ucol(0), ucol(1), ucol(2)],
        out_specs=[tok(D_ATT), tok(LANES)],
        out_shape=[
            jax.ShapeDtypeStruct((b, s, D_ATT), BF16),
            jax.ShapeDtypeStruct((b, s, LANES), F32),
        ],
        scratch_shapes=[
            pltpu.VMEM((2 * sb, D_ATT), BF16), pltpu.VMEM((2 * sb // ATT_BLOCK, D_ATT, ATT_BLOCK), BF16),
            pltpu.VMEM((sb, D_ATT), F32), pltpu.VMEM((D_ATT // LANES, sb, LANES), F32),
            pltpu.VMEM((sb, LANES), F32), pltpu.VMEM((ATT_STREAMS, LANES, ATT_BLOCK), F32),
            pltpu.VMEM((ATT_STREAMS * ATT_SLOTS, ATT_BLOCK, ATT_BLOCK), F32),
            pltpu.VMEM((ATT_STREAMS * ATT_SLOTS, 2 * ATT_BLOCK, ATT_BLOCK), BF16),
            pltpu.VMEM((ATT_STREAMS * ATT_SLOTS, ATT_BLOCK), F32),
            pltpu.VMEM((ATT_STREAMS * ATT_SLOTS, ATT_BLOCK), F32),
        ],
        compiler_params=pltpu.CompilerParams(
            dimension_semantics=("parallel", "arbitrary"), vmem_limit_bytes=VMEM_LIMIT),
        name=f"attn_prompt_g{gi}",
    )(u3, u3, u3)
    return o.reshape(b * s, D_ATT), lse.reshape(b * s, LANES)


def _kv_tail_kernel(x_ref, o_ref, nat_ref, *, win, dil):
    span = min(win, IN_TM)
    per_slab = LANES // ATT_HEAD_DIM
    for c in range(D_ATT // LANES):
        cs = slice(c * LANES, (c + 1) * LANES)
        if dil == 1:
            nat = x_ref[0, :, cs].astype(F32)
        else:
            for t in range(win // span):
                for dst, src, cnt in _residue_pieces(win, dil, span):
                    nat_ref[c, pl.ds(t * span + src, cnt, stride=dil), :] = (
                        x_ref[0, t * span + dst:t * span + dst + cnt, cs].astype(F32))
            nat = nat_ref[c]
        o_ref[0, 0, c * per_slab:(c + 1) * per_slab] = nat.T.reshape(per_slab, ATT_HEAD_DIM, win)


def _kv_tail(u2d, b, s, gi):
    win, dil = ATT_GROUPS[gi]
    assert s % win == 0 and (win <= IN_TM or win % IN_TM == 0)
    col0 = (U_ATT + gi * U_TILE) // D_ATT + 1
    last = s // win - 1
    u3 = u2d.reshape(b, s, N_U)
    return pl.pallas_call(
        functools.partial(_kv_tail_kernel, win=win, dil=dil),
        grid=(b, 2),
        in_specs=[pl.BlockSpec((1, win, D_ATT), lambda bi, kv: (bi, last, col0 + kv))],
        out_specs=pl.BlockSpec((1, 1, ATT_SLOTS, ATT_HEAD_DIM, win), lambda bi, kv: (bi, kv, 0, 0, 0)),
        out_shape=jax.ShapeDtypeStruct((b, 2, ATT_SLOTS, ATT_HEAD_DIM, win), F32),
        scratch_shapes=[pltpu.VMEM((D_ATT // LANES, win, LANES), F32)],
        compiler_params=pltpu.CompilerParams(
            dimension_semantics=("parallel", "parallel"), vmem_limit_bytes=VMEM_LIMIT),
        name=f"kv_tail_g{gi}",
    )(u3)


def _attn_decode_kernel(x_ref, c0_ref, c1_ref, c2_ref, o_ref, n0_ref, n1_ref, n2_ref):
    caches = (c0_ref, c1_ref, c2_ref)
    news = (n0_ref, n1_ref, n2_ref)
    ng = N_ATT_GROUPS

    for hh in range(x_ref.shape[1]):
        outs_g, lses = [], []
        for g, (win, dil) in enumerate(ATT_GROUPS):
            qc = x_ref[0, hh, g]
            kc = x_ref[0, hh, ng + g]
            vc = x_ref[0, hh, 2 * ng + g]
            kt = caches[g][0, 0, hh]
            vt = caches[g][0, 1, hh]
            lane = lax.broadcasted_iota(jnp.int32, (1, win), 1)
            sc = jnp.sum(kt * qc, axis=0, keepdims=True)
            sc = jnp.where(lane % dil == 0, sc, -jnp.inf)
            sc_new = jnp.sum(kc * qc, axis=0, keepdims=True)
            m = jnp.maximum(jnp.max(sc, axis=1, keepdims=True), sc_new)
            p = jnp.exp(sc - m)
            p_new = jnp.exp(sc_new - m)
            den = jnp.sum(p, axis=1, keepdims=True) + p_new
            outs_g.append((jnp.sum(vt * p, axis=1, keepdims=True) + p_new * vc) / den)
            lses.append(m + jnp.log(den))
            last = lane == win - 1
            news[g][0, 0, hh] = jnp.where(last, kc, pltpu.roll(kt, win - 1, axis=1))
            news[g][0, 1, hh] = jnp.where(last, vc, pltpu.roll(vt, win - 1, axis=1))

        top = jnp.maximum(jnp.maximum(lses[0], lses[1]), lses[2])
        ws = [jnp.exp(l - top) for l in lses]
        o_ref[0, hh] = ((ws[0] * outs_g[0] + ws[1] * outs_g[1] + ws[2] * outs_g[2])
                        / (ws[0] + ws[1] + ws[2]))


def _attn_decode(us, caches):
    b = us.shape[0]
    ng = N_ATT_GROUPS
    hd = ATT_HEAD_DIM
    hb = DEC_HEADS_PER_STEP
    x = us[:, U_ATT:].reshape(b, ng, 3, ATT_SLOTS, hd)
    x = x.transpose(0, 3, 2, 1, 4).reshape(b, ATT_SLOTS, 3 * ng, hd, 1)
    cts = [c.transpose(0, 2, 3, 4, 1) for c in caches]
    cache_specs = []
    for c, (win, dil) in zip(cts, ATT_GROUPS):
        assert c.shape == (b, 2, ATT_SLOTS, hd, win) and win // dil == ATT_BLOCK
        cache_specs.append(pl.BlockSpec((1, 2, hb, hd, win), lambda bi, h: (bi, 0, h, 0, 0)))
    res = pl.pallas_call(
        _attn_decode_kernel,
        grid=(b, ATT_SLOTS // hb),
        in_specs=[pl.BlockSpec((1, hb, 3 * ng, hd, 1), lambda bi, h: (bi, h, 0, 0, 0))] + cache_specs,
        out_specs=[pl.BlockSpec((1, hb, hd, 1), lambda bi, h: (bi, h, 0, 0))] + cache_specs,
        out_shape=[jax.ShapeDtypeStruct((b, ATT_SLOTS, hd, 1), F32)]
        + [jax.ShapeDtypeStruct(c.shape, F32) for c in cts],
        compiler_params=pltpu.CompilerParams(
            dimension_semantics=("parallel", "parallel"), vmem_limit_bytes=VMEM_LIMIT),
        name="attn_decode",
    )(x, *cts)
    o = res[0].reshape(b, D_ATT)
    kvs = [r.transpose(0, 4, 1, 2, 3) for r in res[1:]]
    return o, kvs


def _out_kernel(*refs, n_groups):
    o_refs = refs[:n_groups]
    l_refs = refs[n_groups:2 * n_groups - (n_groups == 1)]
    z_ref, br_ref, gs_ref, ga_ref, x_ref, he_ref, wa_ref, wo_ref, y_ref = refs[len(o_refs) + len(l_refs):]
    if n_groups == 1:
        o = o_refs[0][...].astype(F32)
    else:
        ls = [l[...] for l in l_refs]
        top = functools.reduce(jnp.maximum, ls)
        ws = [jnp.exp(l - top) for l in ls]
        den = functools.reduce(lambda a, c: a + c, ws)
        o = None
        for w, o_ref in zip(ws, o_refs):
            term = _split_dot(w / den, he_ref[...], 2) * o_ref[...].astype(F32)
            o = term if o is None else o + term
    o = o * _silu(z_ref[...].astype(F32))
    br_att = jnp.dot(o.astype(BF16), wa_ref[...], preferred_element_type=F32)
    mix = (_sigmoid(gs_ref[...].astype(F32)) * br_ref[...].astype(F32)
           + _sigmoid(ga_ref[...].astype(F32)) * br_att)
    y_ref[...] = x_ref[...] + jnp.dot(mix.astype(BF16), wo_ref[...], preferred_element_type=F32)


def _out_proj(os_, lses, u2d, br_ssd, x2d, p, tm):
    m = x2d.shape[0]
    row = lambda width, off=0: pl.BlockSpec((tm, width), lambda i: (i, off // width))
    return pl.pallas_call(
        functools.partial(_out_kernel, n_groups=len(os_)),
        grid=(m // tm,),
        in_specs=[row(D_ATT)] * len(os_) + [row(LANES)] * len(lses)
        + [row(D_ATT, U_ZATT), row(D_MODEL), row(D_MODEL, U_GSSD), row(D_MODEL, U_GATT), row(D_MODEL),
           _const_spec((LANES, D_ATT)), _const_spec((D_ATT, D_MODEL)), _const_spec((D_MODEL, D_MODEL))],
        out_specs=row(D_MODEL),
        out_shape=jax.ShapeDtypeStruct((m, D_MODEL), F32),
        compiler_params=pltpu.CompilerParams(
            dimension_semantics=("parallel",), vmem_limit_bytes=VMEM_LIMIT),
        name="out_proj",
    )(*os_, *lses, u2d, br_ssd, u2d, u2d, x2d, p["slot_expand"], p["w_att_br"], p["w_out"])


def _rope_tables(pos):
    half = ATT_HEAD_DIM // 2
    inv = ROPE_THETA ** (-jnp.arange(half, dtype=F32) / half)
    ang = pos.astype(F32)[:, None] * inv[None, :]
    cos, sin = jnp.cos(ang), jnp.sin(ang)
    cos = jnp.concatenate([cos, cos], axis=-1)
    sin = jnp.concatenate([-sin, sin], axis=-1)
    return jnp.tile(cos, (1, LANES // ATT_HEAD_DIM)), jnp.tile(sin, (1, LANES // ATT_HEAD_DIM))


def _storage_positions(s, tm):
    out = []
    for win, dil in ATT_GROUPS:
        idx = np.arange(s)
        if dil > 1:
            for t0 in range(0, s, tm):
                for dst, src, cnt in _residue_pieces(win, dil, tm):
                    idx[t0 + dst:t0 + dst + cnt] = t0 + src + dil * np.arange(cnt)
        out.append(idx)
    return np.stack(out)


def _layer_params(l, norm_w, w_in, conv_w, conv_b, dt_bias, a_log, d_skip, ssd_norm_w, q_norm_w,
                  k_norm_w, w_ssd_br, w_att_br, w_out):
    w = w_in[l]
    o_dt = D_SSD + D_XBC
    o_q = o_dt + SSD_HEADS
    o_k, o_v = o_q + D_ATT_QK, o_q + 2 * D_ATT_QK
    o_zatt = o_q + 3 * D_ATT_QK
    o_gssd, o_gatt = o_zatt + D_ATT, o_zatt + D_ATT + D_MODEL
    cols = [w[:, :o_dt], w[:, o_gssd:o_gatt], w[:, o_gatt:o_gatt + D_MODEL], w[:, o_zatt:o_gssd]]
    for g in range(N_ATT_GROUPS):
        cols += [w[:, o + g * D_ATT:o + (g + 1) * D_ATT] for o in (o_q, o_k, o_v)]
    pad = LANES - SSD_HEADS
    lane_i = np.arange(LANES)
    return dict(
        norm_w=norm_w[l][None, :],
        w_main=jnp.concatenate(cols, axis=1).astype(BF16),
        w_dt=jnp.pad(w[:, o_dt:o_q], ((0, 0), (0, pad))).astype(BF16),
        conv_w=conv_w[l], conv_b=conv_b[l][None, :],
        dt_bias=jnp.pad(dt_bias[l].astype(F32), (0, pad))[None, :],
        a_neg=jnp.pad(-jnp.exp(a_log[l].astype(F32)), (0, pad))[None, :],
        d_skip=jnp.repeat(d_skip[l].astype(F32), SSD_HEAD_DIM)[None, :],
        ssd_norm_w=ssd_norm_w[l][None, :],
        head_expand=jnp.asarray(
            lane_i[:, None] == (np.arange(D_SSD) // SSD_HEAD_DIM)[None, :], BF16),
        slot_expand=jnp.asarray(
            lane_i[:, None] == (np.arange(D_ATT) // ATT_HEAD_DIM)[None, :], BF16),
        head_seg=jnp.asarray(
            lane_i[:, None] // ATT_HEAD_DIM == lane_i[None, :] // ATT_HEAD_DIM, BF16),
        qk_norm_w=jnp.concatenate([jnp.tile(q_norm_w[l].astype(F32) * ATT_SCALE, ATT_SLOTS),
                                   jnp.tile(k_norm_w[l].astype(F32), ATT_SLOTS)])[None, :],
        w_ssd_br=w_ssd_br[l].astype(BF16), w_att_br=w_att_br[l].astype(BF16),
        w_out=w_out[l].astype(BF16),
    )


def _prompt_layer(x, p):
    b, s, _ = x.shape
    assert s % ATT_SB == 0 and ATT_SB % IN_TM == 0
    x2d = x.reshape(b * s, D_MODEL)
    order = jnp.asarray(_storage_positions(s, IN_TM).reshape(-1), jnp.int32)
    cos, sin = _rope_tables(order)
    shape3 = (N_ATT_GROUPS, s, LANES)
    u, dt = _inproj(x2d, p, cos.reshape(shape3), sin.reshape(shape3), BF16, IN_TM, True)
    conv0 = jnp.zeros((b, CONV_WIDTH - 1, D_XBC), F32)
    ssm0 = jnp.zeros((b, SSD_HEADS, SSD_HEAD_DIM, SSD_STATE), F32)
    br_ssd, conv_new, ssm_new = _ssd(u.reshape(b, s, N_U), dt.reshape(b, s, LANES), conv0, ssm0, p, BF16)
    os_, lses, kvs = [], [], []
    for gi in range(N_ATT_GROUPS):
        o, lse = _attn_prompt(u, b, s, gi)
        os_.append(o)
        lses.append(lse)
        kvs.append(_kv_tail(u, b, s, gi).transpose(0, 4, 1, 2, 3))
    y = _out_proj(os_, lses, u, br_ssd.reshape(b * s, D_MODEL), x2d, p, 512)
    return y.reshape(b, s, D_MODEL), conv_new, ssm_new, kvs


def _decode_layer(x, conv_state, ssm_state, caches, p):
    b, s, _ = x.shape
    assert s == 1
    x2d = x.reshape(b, D_MODEL)
    cos, sin = _rope_tables(jnp.full((b,), PAST_LEN, jnp.int32))
    cos3 = jnp.broadcast_to(cos[None], (N_ATT_GROUPS, b, LANES))
    sin3 = jnp.broadcast_to(sin[None], (N_ATT_GROUPS, b, LANES))
    u, dt = _inproj(x2d, p, cos3, sin3, F32, b, False)
    br_ssd, conv_new, ssm_new = _ssd(u.reshape(b, s, N_U), dt.reshape(b, s, LANES),
                                     conv_state, ssm_state, p, F32)
    o, kvs = _attn_decode(u, caches)
    y = _out_proj([o], [], u, br_ssd.reshape(b, D_MODEL), x2d, p, b)
    return y.reshape(b, s, D_MODEL), conv_new, ssm_new, kvs


def kernel(x_prompt, x_sample, state_conv, state_ssm, cache_kv_w128, cache_kv_w512, cache_kv_w2048,
           norm_w, w_in, conv_w, conv_b, dt_bias, a_log, d_skip, ssd_norm_w, q_norm_w, k_norm_w,
           w_ssd_br, w_att_br, w_out):
    yp, ys = x_prompt, x_sample
    outs_p = [[] for _ in range(5)]
    outs_s = [[] for _ in range(5)]
    for l in range(norm_w.shape[0]):
        p = _layer_params(l, norm_w, w_in, conv_w, conv_b, dt_bias, a_log, d_skip, ssd_norm_w,
                          q_norm_w, k_norm_w, w_ssd_br, w_att_br, w_out)
        yp, c, h, kv = _prompt_layer(yp, p)
        for acc, val in zip(outs_p, [c, h] + kv):
            acc.append(val)
        ys, c, h, kv = _decode_layer(ys, state_conv[l], state_ssm[l],
                                     [cache_kv_w128[l], cache_kv_w512[l], cache_kv_w2048[l]], p)
        for acc, val in zip(outs_s, [c, h] + kv):
            acc.append(val)
    return (yp, ys, *[jnp.stack(a) for a in outs_p], *[jnp.stack(a) for a in outs_s])
```
